```python
import math
import jax
import jax.numpy as jnp
from jax import lax
import numpy as np

D_MODEL = 1024
BATCH = 4
SEQ = 4096
DEPTH = 4

CTX_LEN = 256
GRID_W = 64
N_EVEN = (DEPTH + 1) // 2
N_ODD = DEPTH // 2
N_MOD = 6
NORM_EPS = 1e-6

D_LRU = D_MODEL // 2
LRU_HEADS = 8
LRU_HEAD_DIM = D_LRU // LRU_HEADS
LRU_CONV_W = 4
LRU_C = 8.0
D_CONV = D_MODEL // 2
CONF_WIDTH = 31
EVEN_IN = 2 * D_LRU + 2 * D_CONV
EVEN_OUT = D_LRU + D_CONV
HEAD_DIM = 128
N_Q_HEADS = D_MODEL // HEAD_DIM
N_KV_HEADS = 2
GQA_GROUP = N_Q_HEADS // N_KV_HEADS
ROT_AXIS_DIM = HEAD_DIM // 2
ROPE_THETA = 10000.0
Q_BLOCK = 128
ODD_IN = (N_Q_HEADS + 2 * N_KV_HEADS) * HEAD_DIM
N_EXPERTS = 16
N_GROUPS = 4
EXPERTS_PER_GROUP = N_EXPERTS // N_GROUPS
TOP_K = 2
D_EXPERT = D_MODEL // 2

kernel_name = 'hybrid_rglru_conformer_gqa_moe_dit'


def rmsnorm(x, g):
    xf = x.astype(jnp.float32)
    y = xf * lax.rsqrt(jnp.mean(xf * xf, axis=-1, keepdims=True) + NORM_EPS)
    return (y * g.astype(jnp.float32)).astype(x.dtype)


def layernorm(x, g, b):
    xf = x.astype(jnp.float32)
    xc = xf - jnp.mean(xf, axis=-1, keepdims=True)
    var = jnp.mean(xc * xc, axis=-1, keepdims=True)
    y = xc * lax.rsqrt(var + NORM_EPS) * g.astype(jnp.float32) + b.astype(jnp.float32)
    return y.astype(x.dtype)


def modulate(h, shift, scale):
    return h * (1.0 + scale) + shift


def depthwise_conv(x, w, b, pad_left):
    k = w.shape[0]
    y = lax.conv_general_dilated(
        x, w[:, None, :].astype(x.dtype), window_strides=(1,),
        padding=[(pad_left, k - 1 - pad_left)],
        dimension_numbers=('NWC', 'WIO', 'NWC'),
        feature_group_count=x.shape[-1])
    return y + b


def _affine_compose(earlier, later):
    a1, b1 = earlier
    a2, b2 = later
    return a1 * a2, a2 * b1 + b2


def linear_scan(a, b, h0, reverse):
    a_cum, b_cum = lax.associative_scan(_affine_compose, (a, b), axis=1, reverse=reverse)
    return b_cum + a_cum * h0[:, None, :]


def rglru_coefficients(u, w_r, b_r, w_i, b_i, lam):
    bsz, t, _ = u.shape
    uh = u.reshape(bsz, t, LRU_HEADS, LRU_HEAD_DIM)
    r = jax.nn.sigmoid(jnp.einsum('bthi,hij->bthj', uh, w_r) + b_r).reshape(bsz, t, D_LRU)
    i = jax.nn.sigmoid(jnp.einsum('bthi,hij->bthj', uh, w_i) + b_i).reshape(bsz, t, D_LRU)
    log_a = -LRU_C * r * jax.nn.softplus(-lam)
    a = jnp.exp(log_a)
    b = jnp.sqrt(-jnp.expm1(2.0 * log_a)) * (i * u)
    return a, b


def rglru_group(x_c, x_l, conv_w, conv_b, w_r, b_r, w_i, b_i, lam):
    u_c = depthwise_conv(x_c, conv_w, conv_b, LRU_CONV_W // 2).astype(jnp.float32)
    u_l = depthwise_conv(x_l, conv_w, conv_b, LRU_CONV_W // 2).astype(jnp.float32)
    h0 = jnp.zeros((x_c.shape[0], D_LRU), jnp.float32)
    y_c = jnp.zeros_like(u_c)
    y_l = jnp.zeros_like(u_l)
    for d, reverse in enumerate((False, True)):
        a_c, b_c = rglru_coefficients(u_c, w_r[d], b_r[d], w_i[d], b_i[d], lam[d])
        h_c = linear_scan(a_c, b_c, h0, reverse)
        h_ctx_final = h_c[:, 0] if reverse else h_c[:, -1]
        a_l, b_l = rglru_coefficients(u_l, w_r[d], b_r[d], w_i[d], b_i[d], lam[d])
        h_l = linear_scan(a_l, b_l, h_ctx_final, reverse)
        y_c = y_c + h_c
        y_l = y_l + h_l
    return y_c, y_l


def conformer_conv_group(u, dw_w, dw_b, ln_g, ln_b):
    val, gate = jnp.split(u, 2, axis=-1)
    v = val * jax.nn.sigmoid(gate)
    v = depthwise_conv(v, dw_w, dw_b, CONF_WIDTH // 2)
    return jax.nn.silu(layernorm(v, ln_g, ln_b))


def recurrent_conv_mixer(h_c, h_l, w_in, w_out, conv_w, conv_b, w_r, b_r, w_i, b_i, lam,
                         dw_w, dw_b, ln_g, ln_b, need_ctx):
    splits = (D_LRU, 2 * D_LRU)
    xa_c, ga_c, ub_c = jnp.split(h_c @ w_in, splits, axis=-1)
    xa_l, ga_l, ub_l = jnp.split(h_l @ w_in, splits, axis=-1)
    s_c, s_l = rglru_group(xa_c, xa_l, conv_w, conv_b, w_r, b_r, w_i, b_i, lam)
    ya_l = s_l.astype(h_l.dtype) * jax.nn.gelu(ga_l)
    yb_l = conformer_conv_group(ub_l, dw_w, dw_b, ln_g, ln_b)
    out_l = jnp.concatenate([ya_l, yb_l], axis=-1) @ w_out
    if not need_ctx:
        return None, out_l
    ya_c = s_c.astype(h_c.dtype) * jax.nn.gelu(ga_c)
    yb_c = conformer_conv_group(ub_c, dw_w, dw_b, ln_g, ln_b)
    out_c = jnp.concatenate([ya_c, yb_c], axis=-1) @ w_out
    return out_c, out_l


def axial_rope_tables(rows):
    r, col = jnp.meshgrid(jnp.arange(rows, dtype=jnp.float32), jnp.arange(GRID_W, dtype=jnp.float32), indexing='ij')
    inv_freq = ROPE_THETA ** (-jnp.arange(0, ROT_AXIS_DIM, 2, dtype=jnp.float32) / ROT_AXIS_DIM)
    ang_r = r.reshape(-1)[:, None] * inv_freq
    ang_c = col.reshape(-1)[:, None] * inv_freq
    return jnp.cos(ang_r), jnp.sin(ang_r), jnp.cos(ang_c), jnp.sin(ang_c)


def rope_axis(x, cos, sin):
    x1, x2 = jnp.split(x, 2, axis=-1)
    cos = cos[None, :, None, :]
    sin = sin[None, :, None, :]
    return jnp.concatenate([x1 * cos - x2 * sin, x2 * cos + x1 * sin], axis=-1)


def axial_rope(x, rope):
    cos_r, sin_r, cos_c, sin_c = rope
    x_row, x_col = jnp.split(x.astype(jnp.float32), 2, axis=-1)
    y = jnp.concatenate([rope_axis(x_row, cos_r, sin_r), rope_axis(x_col, cos_c, sin_c)], axis=-1)
    return y.astype(x.dtype)


def gqa_attend(q, k, v):
    s = jnp.einsum('bqhgd,bshd->bhgqs', q, k).astype(jnp.float32) * (1.0 / math.sqrt(HEAD_DIM))
    p = jax.nn.softmax(s, axis=-1).astype(v.dtype)
    return jnp.einsum('bhgqs,bshd->bqhgd', p, v)


def attention_mixer(h_c, h_l, w_qkv, w_out, q_g, k_g, rope, need_ctx):
    bsz, s_len, _ = h_l.shape
    c_len = h_c.shape[1]

    def project(h):
        t = h.shape[1]
        q, k, v = jnp.split(h @ w_qkv, (N_Q_HEADS * HEAD_DIM, (N_Q_HEADS + N_KV_HEADS) * HEAD_DIM), axis=-1)
        q = rmsnorm(q.reshape(bsz, t, N_Q_HEADS, HEAD_DIM), q_g)
        k = rmsnorm(k.reshape(bsz, t, N_KV_HEADS, HEAD_DIM), k_g)
        return q, k, v.reshape(bsz, t, N_KV_HEADS, HEAD_DIM)

    q_c, k_c, v_c = project(h_c)
    q_l, k_l, v_l = project(h_l)
    q_l = axial_rope(q_l, rope)
    k_l = axial_rope(k_l, rope)
    k_all = jnp.concatenate([k_c, k_l], axis=1)
    v_all = jnp.concatenate([v_c, v_l], axis=1)
    n_blocks = s_len // Q_BLOCK
    q_blocks = q_l.reshape(bsz, n_blocks, Q_BLOCK, N_KV_HEADS, GQA_GROUP, HEAD_DIM).transpose(1, 0, 2, 3, 4, 5)
    o_l = lax.map(lambda qb: gqa_attend(qb, k_all, v_all), q_blocks)
    o_l = o_l.transpose(1, 0, 2, 3, 4, 5).reshape(bsz, s_len, N_Q_HEADS * HEAD_DIM)
    out_l = o_l @ w_out
    if not need_ctx:
        return None, out_l
    o_c = gqa_attend(q_c.reshape(bsz, c_len, N_KV_HEADS, GQA_GROUP, HEAD_DIM), k_c, v_c)
    out_c = o_c.reshape(bsz, c_len, N_Q_HEADS * HEAD_DIM) @ w_out
    return out_c, out_l


def grouped_moe(h, router_w, router_bias, w_gate, w_up, w_down):
    scores = jax.nn.sigmoid((h @ router_w).astype(jnp.float32))
    sel = scores + router_bias.astype(jnp.float32)
    grp_score = lax.top_k(sel.reshape(-1, N_GROUPS, EXPERTS_PER_GROUP), 2)[0].sum(-1)
    best = jnp.argmax(grp_score, axis=-1)
    in_group = (jnp.arange(N_EXPERTS) // EXPERTS_PER_GROUP)[None, :] == best[:, None]
    _, idx = lax.top_k(jnp.where(in_group, sel, -jnp.inf), TOP_K)
    w = jnp.take_along_axis(scores, idx, axis=-1)
    w = w / jnp.sum(w, axis=-1, keepdims=True)
    gates = jnp.sum(jax.nn.one_hot(idx, N_EXPERTS, dtype=jnp.float32) * w[..., None], axis=1)
    out = jnp.zeros_like(h)
    for e in range(N_EXPERTS):
        y = (jax.nn.silu(h @ w_gate[e]) * (h @ w_up[e])) @ w_down[e]
        out = out + gates[:, e:e + 1].astype(h.dtype) * y
    return out


def setup_inputs(seed: int = 0) -> dict:
    key = jax.random.key(seed)
    ks = jax.random.split(key, 32)
    f32 = jnp.float32

    def nrm(k, shape, scale):
        return jax.random.normal(k, shape, f32) * scale

    a0 = jax.random.uniform(ks[14], (N_EVEN, 2, D_LRU), f32, 0.9, 0.999)
    return {
        'x': nrm(ks[0], (BATCH, SEQ, D_MODEL), 1.0),
        'c': nrm(ks[1], (BATCH, D_MODEL), 1.0),
        'ctx': nrm(ks[2], (BATCH, CTX_LEN, D_MODEL), 1.0),
        'c_ctx': nrm(ks[3], (D_MODEL,), 1.0),
        'ada_w': nrm(ks[4], (DEPTH, D_MODEL, N_MOD * D_MODEL), 0.5 * D_MODEL ** -0.5),
        'ada_b': nrm(ks[5], (DEPTH, N_MOD * D_MODEL), 0.02),
        'norm_mix_g': 1.0 + nrm(ks[6], (DEPTH, D_MODEL), 0.1),
        'norm_ffn_g': 1.0 + nrm(ks[7], (DEPTH, D_MODEL), 0.1),
        'ev_w_in': nrm(ks[8], (N_EVEN, D_MODEL, EVEN_IN), D_MODEL ** -0.5),
        'ev_w_out': nrm(ks[9], (N_EVEN, EVEN_OUT, D_MODEL), EVEN_OUT ** -0.5),
        'lru_conv_w': nrm(ks[10], (N_EVEN, LRU_CONV_W, D_LRU), LRU_CONV_W ** -0.5),
        'lru_conv_b': nrm(ks[11], (N_EVEN, D_LRU), 0.02),
        'lru_w_r': nrm(ks[12], (N_EVEN, 2, LRU_HEADS, LRU_HEAD_DIM, LRU_HEAD_DIM), LRU_HEAD_DIM ** -0.5),
        'lru_b_r': nrm(ks[13], (N_EVEN, 2, LRU_HEADS, LRU_HEAD_DIM), 0.1),
        'lru_w_i': nrm(ks[15], (N_EVEN, 2, LRU_HEADS, LRU_HEAD_DIM, LRU_HEAD_DIM), LRU_HEAD_DIM ** -0.5),
        'lru_b_i': nrm(ks[16], (N_EVEN, 2, LRU_HEADS, LRU_HEAD_DIM), 0.1),
        'lru_lambda': jnp.log(a0) - jnp.log1p(-a0),
        'cf_dw_w': nrm(ks[17], (N_EVEN, CONF_WIDTH, D_CONV), CONF_WIDTH ** -0.5),
        'cf_dw_b': nrm(ks[18], (N_EVEN, D_CONV), 0.02),
        'cf_ln_g': 1.0 + nrm(ks[19], (N_EVEN, D_CONV), 0.1),
        'cf_ln_b': nrm(ks[20], (N_EVEN, D_CONV), 0.02),
        'at_w_qkv': nrm(ks[21], (N_ODD, D_MODEL, ODD_IN), D_MODEL ** -0.5),
        'at_w_out': nrm(ks[22], (N_ODD, N_Q_HEADS * HEAD_DIM, D_MODEL), (N_Q_HEADS * HEAD_DIM) ** -0.5),
        'at_q_norm_g': 1.0 + nrm(ks[23], (N_ODD, HEAD_DIM), 0.1),
        'at_k_norm_g': 1.0 + nrm(ks[24], (N_ODD, HEAD_DIM), 0.1),
        'router_w': nrm(ks[25], (D_MODEL, N_EXPERTS), D_MODEL ** -0.5),
        'router_bias': nrm(ks[26], (N_EXPERTS,), 0.01),
        'exp_w_gate': nrm(ks[27], (DEPTH, N_EXPERTS, D_MODEL, D_EXPERT), D_MODEL ** -0.5),
        'exp_w_up': nrm(ks[28], (DEPTH, N_EXPERTS, D_MODEL, D_EXPERT), D_MODEL ** -0.5),
        'exp_w_down': nrm(ks[29], (DEPTH, N_EXPERTS, D_EXPERT, D_MODEL), D_EXPERT ** -0.5),
        'final_norm_g': 1.0 + nrm(ks[30], (D_MODEL,), 0.1),
    }


def reference(x, c, ctx, c_ctx, ada_w, ada_b, norm_mix_g, norm_ffn_g, ev_w_in, ev_w_out,
              lru_conv_w, lru_conv_b, lru_w_r, lru_b_r, lru_w_i, lru_b_i, lru_lambda,
              cf_dw_w, cf_dw_b, cf_ln_g, cf_ln_b, at_w_qkv, at_w_out, at_q_norm_g, at_k_norm_g,
              router_w, router_bias, exp_w_gate, exp_w_up, exp_w_down, final_norm_g):
    bsz, s_len, d = x.shape
    c_len = ctx.shape[1]
    ROWS = s_len // GRID_W
    rope = axial_rope_tables(ROWS)
    silu_c = jax.nn.silu(c)
    silu_cc = jax.nn.silu(c_ctx)
    x_l, x_c = x, ctx
    for l in range(DEPTH):
        need_ctx = l < DEPTH - 1
        mod_l = (silu_c @ ada_w[l] + ada_b[l]).reshape(bsz, 1, N_MOD, d)
        mod_c = (silu_cc @ ada_w[l] + ada_b[l]).reshape(1, 1, N_MOD, d)
        sh1_l, sc1_l, g1_l, sh2_l, sc2_l, g2_l = [mod_l[:, :, i] for i in range(N_MOD)]
        sh1_c, sc1_c, g1_c, sh2_c, sc2_c, g2_c = [mod_c[:, :, i] for i in range(N_MOD)]
        h_l = modulate(rmsnorm(x_l, norm_mix_g[l]), sh1_l, sc1_l)
        h_c = modulate(rmsnorm(x_c, norm_mix_g[l]), sh1_c, sc1_c)
        if l % 2 == 0:
            e = l // 2
            y_c, y_l = recurrent_conv_mixer(
                h_c, h_l, ev_w_in[e], ev_w_out[e], lru_conv_w[e], lru_conv_b[e],
                lru_w_r[e], lru_b_r[e], lru_w_i[e], lru_b_i[e], lru_lambda[e],
                cf_dw_w[e], cf_dw_b[e], cf_ln_g[e], cf_ln_b[e], need_ctx)
        else:
            o = l // 2
            y_c, y_l = attention_mixer(h_c, h_l, at_w_qkv[o], at_w_out[o], at_q_norm_g[o], at_k_norm_g[o],
                                       rope, need_ctx)
        x_l = x_l + g1_l * y_l
        h_l = modulate(rmsnorm(x_l, norm_ffn_g[l]), sh2_l, sc2_l)
        if need_ctx:
            x_c = x_c + g1_c * y_c
            h_c = modulate(rmsnorm(x_c, norm_ffn_g[l]), sh2_c, sc2_c)
            tokens = jnp.concatenate([h_c.reshape(-1, d), h_l.reshape(-1, d)], axis=0)
            f = grouped_moe(tokens, router_w, router_bias, exp_w_gate[l], exp_w_up[l], exp_w_down[l])
            x_c = x_c + g2_c * f[:bsz * c_len].reshape(bsz, c_len, d)
            x_l = x_l + g2_l * f[bsz * c_len:].reshape(bsz, s_len, d)
        else:
            f = grouped_moe(h_l.reshape(-1, d), router_w, router_bias, exp_w_gate[l], exp_w_up[l], exp_w_down[l])
            x_l = x_l + g2_l * f.reshape(bsz, s_len, d)
    return rmsnorm(x_l, final_norm_g)
```

```python
import functools
import math

import jax
import jax.numpy as jnp
from jax import lax
from jax.experimental import pallas as pl
from jax.experimental.pallas import tpu as pltpu

F32 = jnp.float32
BF16 = jnp.bfloat16

D_MODEL = 1024
N_MOD = 6
NORM_EPS = 1e-6
GRID_W = 64

D_LRU = 512
LRU_HEADS = 8
LRU_HEAD_DIM = D_LRU // LRU_HEADS
LRU_CONV_W = 4
LRU_C = 8.0
D_CONV = 512
CONF_WIDTH = 31
CONF_PAD = 16

HEAD_DIM = 128
N_Q_HEADS = 8
N_KV_HEADS = 2
GQA_GROUP = N_Q_HEADS // N_KV_HEADS
ROPE_THETA = 10000.0

N_EXPERTS = 16
N_GROUPS = 4
EXPERTS_PER_GROUP = 4
D_EXPERT = 512

LANES = 128
SUBLANES = 8
TM = 512
CT = 128
TQ = 512
TK = 1024
CONV_CHUNK = 128
LRU_CHUNK = 256
VMEM_LIMIT = 56 * 1024 * 1024


def _cparams(sem):
    return pltpu.CompilerParams(dimension_semantics=sem, vmem_limit_bytes=VMEM_LIMIT)


def _rms(x, g):
    return x * lax.rsqrt(jnp.mean(x * x, axis=-1, keepdims=True) + NORM_EPS) * g


def _split_bf16(a):
    hi = a.astype(BF16)
    lo = (a - hi.astype(F32)).astype(BF16)
    return hi, lo


def _dot(a, b):
    return jnp.dot(a, b, preferred_element_type=F32)


def _dot3(a_hi, a_lo, b_hi, b_lo):
    return _dot(a_hi, b_hi) + (_dot(a_lo, b_hi) + _dot(a_hi, b_lo))


def _mod_row(i, n_lat_tiles, tiles_per_batch, nb):
    return jnp.where(i < n_lat_tiles, i // tiles_per_batch, nb)


def _adaln_kernel(c_ref, w_ref, b_ref, o_ref):
    cv = c_ref[...]
    s = cv * jax.nn.sigmoid(cv)
    s_hi, s_lo = _split_bf16(s)
    w_hi, w_lo = _split_bf16(w_ref[...])
    o_ref[...] = _dot3(s_hi, s_lo, w_hi, w_lo) + b_ref[...]


def _adaln(cvec, ada_w, ada_b, interpret):
    depth = ada_w.shape[0]
    n_out = ada_w.shape[2]
    tn = D_MODEL
    return pl.pallas_call(
        _adaln_kernel,
        out_shape=jax.ShapeDtypeStruct((depth, SUBLANES, n_out), F32),
        grid=(depth, n_out // tn),
        in_specs=[
            pl.BlockSpec((SUBLANES, D_MODEL), lambda l, j: (0, 0)),
            pl.BlockSpec((None, D_MODEL, tn), lambda l, j: (l, 0, j)),
            pl.BlockSpec((None, 1, tn), lambda l, j: (l, 0, j)),
        ],
        out_specs=pl.BlockSpec((None, SUBLANES, tn), lambda l, j: (l, 0, j)),
        compiler_params=_cparams(("arbitrary", "arbitrary")),
        name="adaln_mod",
        interpret=interpret,
    )(cvec, ada_w, ada_b.reshape(depth, 1, n_out))


def _even_in_kernel(x_ref, mod_ref, g_ref, w_ref, o_ref, *, n_lat_tiles, tiles_per_batch, nb):
    i = pl.program_id(0)
    row = _mod_row(i, n_lat_tiles, tiles_per_batch, nb)
    sh = mod_ref[pl.ds(row, 1), 0:D_MODEL]
    sc = mod_ref[pl.ds(row, 1), D_MODEL:2 * D_MODEL]
    h = _rms(x_ref[...], g_ref[...]) * (1.0 + sc) + sh
    u = _dot(h.astype(BF16), w_ref[...])
    o_ref[:, 0:D_LRU] = u[:, 0:D_LRU]
    o_ref[:, D_LRU:2 * D_LRU] = jax.nn.gelu(u[:, D_LRU:2 * D_LRU])
    val = u[:, 2 * D_LRU:2 * D_LRU + D_CONV]
    gate = u[:, 2 * D_LRU + D_CONV:]
    o_ref[:, 2 * D_LRU:] = val * jax.nn.sigmoid(gate)


def _even_in(x, mod_l, g, w_in, dims, interpret):
    nb, s_len, c_len = dims
    t = x.shape[0]
    n_out = 2 * D_LRU + D_CONV
    kern = functools.partial(_even_in_kernel, n_lat_tiles=nb * s_len // TM,
                             tiles_per_batch=s_len // TM, nb=nb)
    return pl.pallas_call(
        kern,
        out_shape=jax.ShapeDtypeStruct((t, n_out), F32),
        grid=(t // TM,),
        in_specs=[
            pl.BlockSpec((TM, D_MODEL), lambda i: (i, 0)),
            pl.BlockSpec(mod_l.shape, lambda i: (0, 0)),
            pl.BlockSpec((1, D_MODEL), lambda i: (0, 0)),
            pl.BlockSpec(w_in.shape, lambda i: (0, 0)),
        ],
        out_specs=pl.BlockSpec((TM, n_out), lambda i: (i, 0)),
        compiler_params=_cparams(("arbitrary",)),
        name="even_in",
        interpret=interpret,
    )(x, mod_l, g.reshape(1, D_MODEL), w_in)


def _block_scan(a, b, reverse):
    n = a.shape[0]
    tmod = lax.broadcasted_iota(jnp.int32, a.shape, 0) & (SUBLANES - 1)
    d = 1
    while d < SUBLANES:
        if reverse:
            keep = tmod < SUBLANES - d
            a_n = pltpu.roll(a, n - d, axis=0)
            b_n = pltpu.roll(b, n - d, axis=0)
        else:
            keep = tmod >= d
            a_n = pltpu.roll(a, d, axis=0)
            b_n = pltpu.roll(b, d, axis=0)
        b = a * jnp.where(keep, b_n, 0.0) + b
        a = a * jnp.where(keep, a_n, 1.0)
        d *= 2
    return a, b


def _lru_coeffs(u, ub, wr, br, wi, bi, nsp):
    r = jax.nn.sigmoid(_dot(ub, wr) + br)
    i = jax.nn.sigmoid(_dot(ub, wi) + bi)
    log_a = (-LRU_C) * r * nsp
    a = jnp.exp(log_a)
    b = jnp.sqrt(1.0 - jnp.exp(2.0 * log_a)) * (i * u)
    return a, b


def _seq_kernel(xl_ref, xc_ref, gl_ref, gc_ref, vl_ref, vc_ref,
                cw_ref, cb_ref, wr_ref, br_ref, wi_ref, bi_ref, sp_ref, dw_ref, db_ref,
                ol_ref, oc_ref,
                af_ref, bf_ref, ar_ref, br_sc_ref, hf_ref, hr_ref, xp_ref, *, n_lru_tiles):
    c = pl.program_id(1)
    s_len = xl_ref.shape[0]
    c_len = xc_ref.shape[0]

    @pl.when(c < n_lru_tiles)
    def _lru():
        cw = cw_ref[...]
        cb = cb_ref[...]

        def coeffs(x_ref, n):
            zpad = jnp.zeros((SUBLANES, CT), F32)
            xp_ref[0:SUBLANES] = zpad
            xp_ref[SUBLANES:SUBLANES + n] = x_ref[...]
            xp_ref[SUBLANES + n:2 * SUBLANES + n] = zpad
            ch = min(LRU_CHUNK, n)

            def body(j, _):
                t0 = pl.multiple_of(j * ch, ch)
                win = xp_ref[pl.ds(t0, ch + 2 * SUBLANES)]
                off = SUBLANES - LRU_CONV_W // 2
                u = cb
                for k in range(LRU_CONV_W):
                    u = u + cw[k:k + 1] * win[off + k:off + k + ch]
                ub = u.astype(BF16)
                a_f, b_f = _lru_coeffs(u, ub, wr_ref[0], br_ref[0], wi_ref[0], bi_ref[0], sp_ref[0])
                a_f, b_f = _block_scan(a_f, b_f, False)
                af_ref[pl.ds(t0, ch)] = a_f
                bf_ref[pl.ds(t0, ch)] = b_f
                a_r, b_r = _lru_coeffs(u, ub, wr_ref[1], br_ref[1], wi_ref[1], bi_ref[1], sp_ref[1])
                a_r, b_r = _block_scan(a_r, b_r, True)
                ar_ref[pl.ds(t0, ch)] = a_r
                br_sc_ref[pl.ds(t0, ch)] = b_r
                return 0

            lax.fori_loop(0, n // ch, body, 0)

        def carry_pass(n, h0f, h0r):
            nblk = n // SUBLANES

            def body(k, carry):
                hf, hr = carry
                kf = pl.multiple_of(k * SUBLANES, SUBLANES)
                kr = pl.multiple_of((nblk - 1 - k) * SUBLANES, SUBLANES)
                yf = bf_ref[pl.ds(kf, SUBLANES)] + af_ref[pl.ds(kf, SUBLANES)] * hf
                yr = br_sc_ref[pl.ds(kr, SUBLANES)] + ar_ref[pl.ds(kr, SUBLANES)] * hr
                hf_ref[pl.ds(kf, SUBLANES)] = yf
                hr_ref[pl.ds(kr, SUBLANES)] = yr
                return yf[SUBLANES - 1:SUBLANES], yr[0:1]

            return lax.fori_loop(0, nblk, body, (h0f, h0r))

        zero = jnp.zeros((1, CT), F32)
        coeffs(xc_ref, c_len)
        hcf, hcr = carry_pass(c_len, zero, zero)
        oc_ref[...] = (hf_ref[0:c_len] + hr_ref[0:c_len]) * gc_ref[...]
        coeffs(xl_ref, s_len)
        carry_pass(s_len, hcf, hcr)
        ol_ref[...] = (hf_ref[0:s_len] + hr_ref[0:s_len]) * gl_ref[...]

    @pl.when(c >= n_lru_tiles)
    def _conv():
        dw = dw_ref[...]
        db = db_ref[...]
        zpad = jnp.zeros((CONF_PAD, CT), F32)

        def conv(v_ref, o_ref, n):
            xp_ref[0:CONF_PAD] = zpad
            xp_ref[CONF_PAD:CONF_PAD + n] = v_ref[...]
            xp_ref[CONF_PAD + n:2 * CONF_PAD + n] = zpad

            def body(j, _):
                t0 = pl.multiple_of(j * CONV_CHUNK, CONV_CHUNK)
                win = xp_ref[pl.ds(t0, CONV_CHUNK + 2 * CONF_PAD)]
                acc = jnp.broadcast_to(db, (CONV_CHUNK, CT))
                off = CONF_PAD - CONF_WIDTH // 2
                for k in range(CONF_WIDTH):
                    acc = acc + dw[k:k + 1] * win[off + k:off + k + CONV_CHUNK]
                o_ref[pl.ds(t0, CONV_CHUNK)] = acc
                return 0

            lax.fori_loop(0, n // CONV_CHUNK, body, 0)

        conv(vc_ref, oc_ref, c_len)
        conv(vl_ref, ol_ref, s_len)


def _seq_mix(u_all, p, dims, interpret):
    nb, s_len, c_len = dims
    n_lru = D_LRU // CT
    n_cv = D_CONV // CT
    lat_blocks = nb * s_len // c_len

    def lru_c(c):
        return jnp.minimum(c, n_lru - 1)

    def cv_c(c):
        return jnp.maximum(c - n_lru, 0)

    in_specs = [
        pl.BlockSpec((s_len, CT), lambda b, c: (b, lru_c(c))),
        pl.BlockSpec((c_len, CT), lambda b, c: (lat_blocks + b, lru_c(c))),
        pl.BlockSpec((s_len, CT), lambda b, c: (b, n_lru + lru_c(c))),
        pl.BlockSpec((c_len, CT), lambda b, c: (lat_blocks + b, n_lru + lru_c(c))),
        pl.BlockSpec((s_len, CT), lambda b, c: (b, 2 * n_lru + cv_c(c))),
        pl.BlockSpec((c_len, CT), lambda b, c: (lat_blocks + b, 2 * n_lru + cv_c(c))),
        pl.BlockSpec((LRU_CONV_W, CT), lambda b, c: (0, lru_c(c))),
        pl.BlockSpec((1, CT), lambda b, c: (0, lru_c(c))),
        pl.BlockSpec((2, None, CT, CT), lambda b, c: (0, lru_c(c), 0, 0)),
        pl.BlockSpec((2, 1, CT), lambda b, c: (0, 0, lru_c(c))),
        pl.BlockSpec((2, None, CT, CT), lambda b, c: (0, lru_c(c), 0, 0)),
        pl.BlockSpec((2, 1, CT), lambda b, c: (0, 0, lru_c(c))),
        pl.BlockSpec((2, 1, CT), lambda b, c: (0, 0, lru_c(c))),
        pl.BlockSpec((CONF_WIDTH, CT), lambda b, c: (0, cv_c(c))),
        pl.BlockSpec((1, CT), lambda b, c: (0, cv_c(c))),
    ]
    out_specs = [
        pl.BlockSpec((s_len, CT), lambda b, c: (b, c)),
        pl.BlockSpec((c_len, CT), lambda b, c: (b, c)),
    ]
    scratch = [pltpu.VMEM((s_len, CT), F32) for _ in range(6)]
    scratch.append(pltpu.VMEM((s_len + 2 * CONF_PAD, CT), F32))
    return pl.pallas_call(
        functools.partial(_seq_kernel, n_lru_tiles=n_lru),
        out_shape=[jax.ShapeDtypeStruct((nb * s_len, D_LRU + D_CONV), F32),
                   jax.ShapeDtypeStruct((nb * c_len, D_LRU + D_CONV), F32)],
        grid=(nb, n_lru + n_cv),
        in_specs=in_specs,
        out_specs=out_specs,
        scratch_shapes=scratch,
        compiler_params=_cparams(("arbitrary", "arbitrary")),
        name="seq_mix",
        interpret=interpret,
    )(u_all, u_all, u_all, u_all, u_all, u_all,
      p["conv_w"], p["conv_b"], p["w_r"], p["b_r"], p["w_i"], p["b_i"], p["nsp"], p["dw_w"], p["dw_b"])


def _swap32(x):
    lane = lax.broadcasted_iota(jnp.int32, x.shape, 1)
    return jnp.where((lane & 32) == 0, pltpu.roll(x, HEAD_DIM - 32, axis=1), pltpu.roll(x, 32, axis=1))


def _odd_in_kernel(x_ref, mod_ref, g_ref, w_ref, qg_ref, kg_ref, cos_ref, sin_ref, o_ref,
                   *, n_lat_tiles, tiles_per_batch, nb):
    i = pl.program_id(0)
    row = _mod_row(i, n_lat_tiles, tiles_per_batch, nb)
    sh = mod_ref[pl.ds(row, 1), 0:D_MODEL]
    sc = mod_ref[pl.ds(row, 1), D_MODEL:2 * D_MODEL]
    h = _rms(x_ref[...], g_ref[...]) * (1.0 + sc) + sh
    u = _dot(h.astype(BF16), w_ref[...])
    is_lat = i < n_lat_tiles
    cos = jnp.where(is_lat, cos_ref[...], 1.0)
    sin = jnp.where(is_lat, sin_ref[...], 0.0)
    q_scale = 1.0 / math.sqrt(HEAD_DIM)
    for hh in range(N_Q_HEADS + N_KV_HEADS):
        xh = u[:, hh * HEAD_DIM:(hh + 1) * HEAD_DIM]
        gain = qg_ref[...] if hh < N_Q_HEADS else kg_ref[...]
        xn = _rms(xh, gain)
        y = xn * cos + _swap32(xn) * sin
        if hh < N_Q_HEADS:
            y = y * q_scale
        o_ref[:, hh * HEAD_DIM:(hh + 1) * HEAD_DIM] = y.astype(BF16)
    v0 = (N_Q_HEADS + N_KV_HEADS) * HEAD_DIM
    o_ref[:, v0:] = u[:, v0:].astype(BF16)


def _odd_in(x, mod_l, g, w_qkv, q_g, k_g, rope_cos, rope_sin, dims, interpret):
    nb, s_len, c_len = dims
    t = x.shape[0]
    n_out = w_qkv.shape[1]
    tpb = s_len // TM
    n_lat = nb * tpb
    kern = functools.partial(_odd_in_kernel, n_lat_tiles=n_lat, tiles_per_batch=tpb, nb=nb)

    def pos_block(i):
        return jnp.where(i < n_lat, i % tpb, 0)

    return pl.pallas_call(
        kern,
        out_shape=jax.ShapeDtypeStruct((t, n_out), BF16),
        grid=(t // TM,),
        in_specs=[
            pl.BlockSpec((TM, D_MODEL), lambda i: (i, 0)),
            pl.BlockSpec(mod_l.shape, lambda i: (0, 0)),
            pl.BlockSpec((1, D_MODEL), lambda i: (0, 0)),
            pl.BlockSpec(w_qkv.shape, lambda i: (0, 0)),
            pl.BlockSpec((1, HEAD_DIM), lambda i: (0, 0)),
            pl.BlockSpec((1, HEAD_DIM), lambda i: (0, 0)),
            pl.BlockSpec((TM, HEAD_DIM), lambda i: (pos_block(i), 0)),
            pl.BlockSpec((TM, HEAD_DIM), lambda i: (pos_block(i), 0)),
        ],
        out_specs=pl.BlockSpec((TM, n_out), lambda i: (i, 0)),
        compiler_params=_cparams(("arbitrary",)),
        name="odd_in",
        interpret=interpret,
    )(x, mod_l, g.reshape(1, D_MODEL), w_qkv, q_g.reshape(1, HEAD_DIM), k_g.reshape(1, HEAD_DIM),
      rope_cos, rope_sin)


def _attn_kernel(*refs, n_seg):
    q_ref = refs[0]
    k_refs = refs[1:1 + n_seg]
    v_refs = refs[1 + n_seg:1 + 2 * n_seg]
    o_ref = refs[1 + 2 * n_seg]
    tq = q_ref.shape[0]
    chunks = []
    for k_ref, v_ref in zip(k_refs, v_refs):
        n = k_ref.shape[0]
        step = min(TK, n)
        for s0 in range(0, n, step):
            chunks.append((k_ref, v_ref, s0, step))
    for g in range(GQA_GROUP):
        q = q_ref[:, g * HEAD_DIM:(g + 1) * HEAD_DIM]
        m = jnp.full((tq, 1), -jnp.inf, F32)
        l = jnp.zeros((tq, 1), F32)
        acc = jnp.zeros((tq, HEAD_DIM), F32)
        for k_ref, v_ref, s0, step in chunks:
            k = k_ref[s0:s0 + step, :]
            v = v_ref[s0:s0 + step, :]
            s = lax.dot_general(q, k, (((1,), (1,)), ((), ())), preferred_element_type=F32)
            m_new = jnp.maximum(m, jnp.max(s, axis=-1, keepdims=True))
            alpha = jnp.exp(m - m_new)
            p = jnp.exp(s - m_new)
            l = alpha * l + jnp.sum(p, axis=-1, keepdims=True)
            acc = alpha * acc + _dot(p.astype(BF16), v)
            m = m_new
        o_ref[:, g * HEAD_DIM:(g + 1) * HEAD_DIM] = (acc / l).astype(BF16)


def _attention(qkv, dims, latent, interpret):
    nb, s_len, c_len = dims
    k_col = N_Q_HEADS
    v_col = N_Q_HEADS + N_KV_HEADS
    lat_blocks = nb * s_len // c_len
    ctx_k = pl.BlockSpec((c_len, HEAD_DIM), lambda b, h, qi: (lat_blocks + b, k_col + h))
    ctx_v = pl.BlockSpec((c_len, HEAD_DIM), lambda b, h, qi: (lat_blocks + b, v_col + h))
    if latent:
        tq = TQ
        nq = s_len // tq
        q_spec = pl.BlockSpec((tq, GQA_GROUP * HEAD_DIM), lambda b, h, qi: (b * nq + qi, h))
        k_specs = [pl.BlockSpec((s_len, HEAD_DIM), lambda b, h, qi: (b, k_col + h)), ctx_k]
        v_specs = [pl.BlockSpec((s_len, HEAD_DIM), lambda b, h, qi: (b, v_col + h)), ctx_v]
        out_rows = nb * s_len
        o_spec = pl.BlockSpec((tq, GQA_GROUP * HEAD_DIM), lambda b, h, qi: (b * nq + qi, h))
    else:
        tq = c_len
        nq = 1
        q_spec = pl.BlockSpec((tq, GQA_GROUP * HEAD_DIM), lambda b, h, qi: (lat_blocks + b, h))
        k_specs = [ctx_k]
        v_specs = [ctx_v]
        out_rows = nb * c_len
        o_spec = pl.BlockSpec((tq, GQA_GROUP * HEAD_DIM), lambda b, h, qi: (b, h))
    n_seg = len(k_specs)
    return pl.pallas_call(
        functools.partial(_attn_kernel, n_seg=n_seg),
        out_shape=jax.ShapeDtypeStruct((out_rows, N_Q_HEADS * HEAD_DIM), BF16),
        grid=(nb, N_KV_HEADS, nq),
        in_specs=[q_spec] + k_specs + v_specs,
        out_specs=o_spec,
        compiler_params=_cparams(("arbitrary", "arbitrary", "arbitrary")),
        name="attn_lat" if latent else "attn_ctx",
        interpret=interpret,
    )(*([qkv] * (1 + 2 * n_seg)))


def _route(logits_t, bias):
    score = [jax.nn.sigmoid(logits_t[e:e + 1]) for e in range(N_EXPERTS)]
    sel = [score[e] + bias[e] for e in range(N_EXPERTS)]
    grp = []
    for g in range(N_GROUPS):
        a, b, c, d = sel[4 * g:4 * g + 4]
        hi1, lo1 = jnp.maximum(a, b), jnp.minimum(a, b)
        hi2, lo2 = jnp.maximum(c, d), jnp.minimum(c, d)
        top1 = jnp.maximum(hi1, hi2)
        top2 = jnp.maximum(jnp.minimum(hi1, hi2), jnp.maximum(lo1, lo2))
        grp.append(top1 + top2)
    best = jnp.zeros_like(grp[0], dtype=jnp.int32)
    best_s = grp[0]
    for g in range(1, N_GROUPS):
        better = grp[g] > best_s
        best = jnp.where(better, g, best)
        best_s = jnp.where(better, grp[g], best_s)
    chosen = []
    for e in range(N_EXPERTS):
        g = e // EXPERTS_PER_GROUP
        beaten = jnp.zeros_like(best)
        for k in range(4 * g, 4 * g + 4):
            if k == e:
                continue
            beats = (sel[k] >= sel[e]) if k < e else (sel[k] > sel[e])
            beaten = beaten + jnp.where(beats, 1, 0)
        chosen.append((beaten < 2) & (best == g))
    wsum = jnp.zeros_like(score[0])
    for e in range(N_EXPERTS):
        wsum = wsum + jnp.where(chosen[e], score[e], 0.0)
    return [jnp.where(chosen[e], score[e] / wsum, 0.0) for e in range(N_EXPERTS)]


def _mix_out_kernel(yl_ref, yc_ref, x_ref, mod_ref, w_ref, lng_ref, lnb_ref, nfg_ref, rwh_ref, rwl_ref, rb_ref,
                    xo_ref, h2_ref, gates_ref, *, even, n_lat_tiles, tiles_per_batch, nb):
    i = pl.program_id(0)
    row = _mod_row(i, n_lat_tiles, tiles_per_batch, nb)
    g1 = mod_ref[pl.ds(row, 1), 2 * D_MODEL:3 * D_MODEL]
    sh2 = mod_ref[pl.ds(row, 1), 3 * D_MODEL:4 * D_MODEL]
    sc2 = mod_ref[pl.ds(row, 1), 4 * D_MODEL:5 * D_MODEL]
    is_lat = i < n_lat_tiles
    y = jnp.where(is_lat, yl_ref[...], yc_ref[...])
    if even:
        ya = y[:, 0:D_LRU]
        vc = y[:, D_LRU:]
        mu = jnp.mean(vc, axis=-1, keepdims=True)
        xc = vc - mu
        var = jnp.mean(xc * xc, axis=-1, keepdims=True)
        ln = xc * lax.rsqrt(var + NORM_EPS) * lng_ref[...] + lnb_ref[...]
        yb = ln * jax.nn.sigmoid(ln)
        out = _dot(ya.astype(BF16), w_ref[0:D_LRU, :]) + _dot(yb.astype(BF16), w_ref[D_LRU:, :])
    else:
        out = _dot(y, w_ref[...])
    x_new = x_ref[...] + g1 * out
    xo_ref[...] = x_new
    h2 = _rms(x_new, nfg_ref[...]) * (1.0 + sc2) + sh2
    h2_ref[...] = h2.astype(BF16)
    h_hi, h_lo = _split_bf16(h2)
    logits = _dot3(h_hi, h_lo, rwh_ref[...], rwl_ref[...])
    logits_t = jnp.transpose(logits)[0:N_EXPERTS]
    gate_rows = _route(logits_t, [rb_ref[e] for e in range(N_EXPERTS)])
    tm = logits.shape[0]
    sub = lax.broadcasted_iota(jnp.int32, (N_EXPERTS, tm), 0)
    gates_t = jnp.zeros((N_EXPERTS, tm), F32)
    for e in range(N_EXPERTS):
        gates_t = jnp.where(sub == e, gate_rows[e], gates_t)
    gates_t = jnp.concatenate([gates_t, jnp.zeros((LANES - N_EXPERTS, tm), F32)], axis=0)
    gates_ref[...] = jnp.transpose(gates_t)


def _mix_out(y_lat, y_ctx, x, mod_l, w_out, ln_g, ln_b, nf_g, rw_hi, rw_lo, router_bias, dims, n_tiles, even,
             interpret):
    nb, s_len, c_len = dims
    tpb = s_len // TM
    n_lat = nb * tpb
    n_ctx = max(nb * c_len // TM, 1)
    rows = n_tiles * TM
    kern = functools.partial(_mix_out_kernel, even=even, n_lat_tiles=n_lat, tiles_per_batch=tpb, nb=nb)
    vec = lambda n: pl.BlockSpec((1, n), lambda i: (0, 0))
    return pl.pallas_call(
        kern,
        out_shape=[jax.ShapeDtypeStruct((rows, D_MODEL), F32),
                   jax.ShapeDtypeStruct((rows, D_MODEL), BF16),
                   jax.ShapeDtypeStruct((rows, LANES), F32)],
        grid=(n_tiles,),
        in_specs=[
            pl.BlockSpec((TM, D_MODEL), lambda i: (jnp.minimum(i, n_lat - 1), 0)),
            pl.BlockSpec((TM, D_MODEL), lambda i: (jnp.clip(i - n_lat, 0, n_ctx - 1), 0)),
            pl.BlockSpec((TM, D_MODEL), lambda i: (i, 0)),
            pl.BlockSpec(mod_l.shape, lambda i: (0, 0)),
            pl.BlockSpec(w_out.shape, lambda i: (0, 0)),
            vec(D_CONV), vec(D_CONV), vec(D_MODEL),
            pl.BlockSpec(rw_hi.shape, lambda i: (0, 0)),
            pl.BlockSpec(rw_lo.shape, lambda i: (0, 0)),
            pl.BlockSpec(memory_space=pltpu.SMEM),
        ],
        out_specs=[
            pl.BlockSpec((TM, D_MODEL), lambda i: (i, 0)),
            pl.BlockSpec((TM, D_MODEL), lambda i: (i, 0)),
            pl.BlockSpec((TM, LANES), lambda i: (i, 0)),
        ],
        compiler_params=_cparams(("arbitrary",)),
        name="mix_out_even" if even else "mix_out_odd",
        interpret=interpret,
    )(y_lat, y_ctx, x, mod_l, w_out, ln_g.reshape(1, D_CONV), ln_b.reshape(1, D_CONV),
      nf_g.reshape(1, D_MODEL), rw_hi, rw_lo, router_bias)


def _moe_kernel(h_ref, gates_ref, x_ref, mod_ref, wg_ref, wu_ref, wd_ref, fg_ref, o_ref, acc_ref,
                *, final, n_lat_tiles, tiles_per_batch, nb):
    i = pl.program_id(0)
    e = pl.program_id(1)

    @pl.when(e == 0)
    def _init():
        acc_ref[...] = jnp.zeros_like(acc_ref)

    h = h_ref[...]
    lane = lax.broadcasted_iota(jnp.int32, gates_ref.shape, 1)
    gate = jnp.sum(jnp.where(lane == e, gates_ref[...], 0.0), axis=-1, keepdims=True)
    a = _dot(h, wg_ref[...])
    u = _dot(h, wu_ref[...])
    hid = (a * jax.nn.sigmoid(a)) * u
    y = _dot(hid.astype(BF16), wd_ref[...])
    acc_ref[...] += gate * y

    @pl.when(e == pl.num_programs(1) - 1)
    def _fin():
        row = _mod_row(i, n_lat_tiles, tiles_per_batch, nb)
        g2 = mod_ref[pl.ds(row, 1), 5 * D_MODEL:6 * D_MODEL]
        x_new = x_ref[...] + g2 * acc_ref[...]
        if final:
            x_new = _rms(x_new, fg_ref[...])
        o_ref[...] = x_new


def _moe(h2, gates, x, mod_l, wg, wu, wd, final_g, dims, n_tiles, final, interpret):
    nb, s_len, c_len = dims
    tpb = s_len // TM
    kern = functools.partial(_moe_kernel, final=final, n_lat_tiles=nb * tpb, tiles_per_batch=tpb, nb=nb)
    return pl.pallas_call(
        kern,
        out_shape=jax.ShapeDtypeStruct((n_tiles * TM, D_MODEL), F32),
        grid=(n_tiles, N_EXPERTS),
        in_specs=[
            pl.BlockSpec((TM, D_MODEL), lambda i, e: (i, 0)),
            pl.BlockSpec((TM, LANES), lambda i, e: (i, 0)),
            pl.BlockSpec((TM, D_MODEL), lambda i, e: (i, 0)),
            pl.BlockSpec(mod_l.shape, lambda i, e: (0, 0)),
            pl.BlockSpec((None, D_MODEL, D_EXPERT), lambda i, e: (e, 0, 0)),
            pl.BlockSpec((None, D_MODEL, D_EXPERT), lambda i, e: (e, 0, 0)),
            pl.BlockSpec((None, D_EXPERT, D_MODEL), lambda i, e: (e, 0, 0)),
            pl.BlockSpec((1, D_MODEL), lambda i, e: (0, 0)),
        ],
        out_specs=pl.BlockSpec((TM, D_MODEL), lambda i, e: (i, 0)),
        scratch_shapes=[pltpu.VMEM((TM, D_MODEL), F32)],
        compiler_params=_cparams(("arbitrary", "arbitrary")),
        name="moe_final" if final else "moe",
        interpret=interpret,
    )(h2, gates, x, mod_l, wg, wu, wd, final_g.reshape(1, D_MODEL))


def _rope_tables(s_len):
    rows = s_len // GRID_W
    quarter = HEAD_DIM // 4
    r, col = jnp.meshgrid(jnp.arange(rows, dtype=F32), jnp.arange(GRID_W, dtype=F32), indexing="ij")
    inv_freq = ROPE_THETA ** (-jnp.arange(0, HEAD_DIM // 2, 2, dtype=F32) / (HEAD_DIM // 2))
    ang_r = r.reshape(-1)[:, None] * inv_freq
    ang_c = col.reshape(-1)[:, None] * inv_freq
    cos = jnp.concatenate([jnp.cos(ang_r)] * 2 + [jnp.cos(ang_c)] * 2, axis=-1)
    sin = jnp.concatenate([-jnp.sin(ang_r), jnp.sin(ang_r), -jnp.sin(ang_c), jnp.sin(ang_c)], axis=-1)
    assert cos.shape == (s_len, 4 * quarter)
    return cos, sin


def _block_diag_tiles(w):
    per = CT // LRU_HEAD_DIM
    n_tiles = D_LRU // CT
    w = w.reshape(2, n_tiles, per, LRU_HEAD_DIM, LRU_HEAD_DIM)
    eye = jnp.eye(per, dtype=w.dtype)
    full = jnp.einsum("dtpij,pq->dtpiqj", w, eye)
    return full.reshape(2, n_tiles, CT, CT).astype(BF16)


def _forward(x, c, ctx, c_ctx, ada_w, ada_b, norm_mix_g, norm_ffn_g, ev_w_in, ev_w_out,
             lru_conv_w, lru_conv_b, lru_w_r, lru_b_r, lru_w_i, lru_b_i, lru_lambda,
             cf_dw_w, cf_dw_b, cf_ln_g, cf_ln_b, at_w_qkv, at_w_out, at_q_norm_g, at_k_norm_g,
             router_w, router_bias, exp_w_gate, exp_w_up, exp_w_down, final_norm_g, interpret=False):
    nb, s_len, d = x.shape
    c_len = ctx.shape[1]
    depth = ada_w.shape[0]
    dims = (nb, s_len, c_len)
    assert d == D_MODEL and nb < SUBLANES
    assert s_len % TM == 0 and (nb * c_len) % TM == 0 and s_len % c_len == 0 and s_len % TQ == 0
    n_lat_tiles = nb * s_len // TM
    n_all_tiles = n_lat_tiles + nb * c_len // TM

    cvec = jnp.zeros((SUBLANES, d), F32).at[:nb].set(c).at[nb].set(c_ctx)
    mod = _adaln(cvec, ada_w, ada_b, interpret)
    xf = jnp.concatenate([x.reshape(nb * s_len, d), ctx.reshape(nb * c_len, d)], axis=0)

    rope_cos, rope_sin = _rope_tables(s_len)
    rw = jnp.zeros((d, LANES), F32).at[:, :N_EXPERTS].set(router_w)
    rw_hi, rw_lo = _split_bf16(rw)
    rb = router_bias.astype(F32)

    for l in range(depth):
        last = l == depth - 1
        n_tiles = n_lat_tiles if last else n_all_tiles
        mod_l = mod[l]
        if l % 2 == 0:
            e = l // 2
            u_all = _even_in(xf, mod_l, norm_mix_g[l], ev_w_in[e].astype(BF16), dims, interpret)
            p = {
                "conv_w": lru_conv_w[e], "conv_b": lru_conv_b[e].reshape(1, D_LRU),
                "w_r": _block_diag_tiles(lru_w_r[e]), "b_r": lru_b_r[e].reshape(2, 1, D_LRU),
                "w_i": _block_diag_tiles(lru_w_i[e]), "b_i": lru_b_i[e].reshape(2, 1, D_LRU),
                "nsp": jax.nn.softplus(-lru_lambda[e]).reshape(2, 1, D_LRU),
                "dw_w": cf_dw_w[e], "dw_b": cf_dw_b[e].reshape(1, D_CONV),
            }
            y_lat, y_ctx = _seq_mix(u_all, p, dims, interpret)
            w_out = ev_w_out[e].astype(BF16)
            ln_g, ln_b = cf_ln_g[e], cf_ln_b[e]
        else:
            o = l // 2
            qkv = _odd_in(xf, mod_l, norm_mix_g[l], at_w_qkv[o].astype(BF16), at_q_norm_g[o], at_k_norm_g[o],
                          rope_cos, rope_sin, dims, interpret)
            y_lat = _attention(qkv, dims, True, interpret)
            y_ctx = y_lat if last else _attention(qkv, dims, False, interpret)
            w_out = at_w_out[o].astype(BF16)
            ln_g, ln_b = cf_ln_g[0], cf_ln_b[0]
        x_mid, h2, gates = _mix_out(y_lat, y_ctx, xf, mod_l, w_out, ln_g, ln_b, norm_ffn_g[l], rw_hi, rw_lo, rb,
                                    dims, n_tiles, l % 2 == 0, interpret)
        xf = _moe(h2, gates, x_mid, mod_l, exp_w_gate[l].astype(BF16), exp_w_up[l].astype(BF16),
                  exp_w_down[l].astype(BF16), final_norm_g, dims, n_tiles, last, interpret)
    return xf.reshape(nb, s_len, d)


def kernel(x, c, ctx, c_ctx, ada_w, ada_b, norm_mix_g, norm_ffn_g, ev_w_in, ev_w_out, lru_conv_w, lru_conv_b,
           lru_w_r, lru_b_r, lru_w_i, lru_b_i, lru_lambda, cf_dw_w, cf_dw_b, cf_ln_g, cf_ln_b, at_w_qkv, at_w_out,
           at_q_norm_g, at_k_norm_g, router_w, router_bias, exp_w_gate, exp_w_up, exp_w_down, final_norm_g):
    return _forward(x, c, ctx, c_ctx, ada_w, ada_b, norm_mix_g, norm_ffn_g, ev_w_in, ev_w_out, lru_conv_w,
                    lru_conv_b, lru_w_r, lru_b_r, lru_w_i, lru_b_i, lru_lambda, cf_dw_w, cf_dw_b, cf_ln_g,
                    cf_ln_b, at_w_qkv, at_w_out, at_q_norm_g, at_k_norm_g, router_w, router_bias, exp_w_gate,
                    exp_w_up, exp_w_down, final_norm_g)
```

```python
import functools
import math

import jax
import jax.numpy as jnp
from jax import lax
from jax.experimental import pallas as pl
from jax.experimental.pallas import tpu as pltpu

F32 = jnp.float32
BF16 = jnp.bfloat16

D_MODEL = 1024
N_MOD = 6
NORM_EPS = 1e-6
GRID_W = 64

D_LRU = 512
LRU_HEADS = 8
LRU_HEAD_DIM = D_LRU // LRU_HEADS
LRU_CONV_W = 4
LRU_C = 8.0
D_CONV = 512
CONF_WIDTH = 31
CONF_PAD = 16

HEAD_DIM = 128
N_Q_HEADS = 8
N_KV_HEADS = 2
GQA_GROUP = N_Q_HEADS // N_KV_HEADS
ROPE_THETA = 10000.0

N_EXPERTS = 16
N_GROUPS = 4
EXPERTS_PER_GROUP = 4
D_EXPERT = 512

LANES = 128
SUBLANES = 8
TM = 512
TMS = 512
ROW_W = D_MODEL + LANES
CT = 128
TQ = 512
TK = 1024
CONV_CHUNK = 128
LRU_CHUNK = 256
VMEM_LIMIT = 56 * 1024 * 1024


def _cparams(sem):
    return pltpu.CompilerParams(dimension_semantics=sem, vmem_limit_bytes=VMEM_LIMIT)


def _rms(x, g):
    return x * lax.rsqrt(jnp.mean(x * x, axis=-1, keepdims=True) + NORM_EPS) * g


def _split_bf16(a):
    hi = a.astype(BF16)
    lo = (a - hi.astype(F32)).astype(BF16)
    return hi, lo


def _dot(a, b):
    return jnp.dot(a, b, preferred_element_type=F32)


def _dot3(a_hi, a_lo, b_hi, b_lo):
    return _dot(a_hi, b_hi) + (_dot(a_lo, b_hi) + _dot(a_hi, b_lo))


def _mod_row(i, n_lat_tiles, tiles_per_batch, nb):
    return jnp.where(i < n_lat_tiles, i // tiles_per_batch, nb)


def _adaln_kernel(c_ref, w_ref, b_ref, o_ref):
    cv = c_ref[...]
    s = cv * jax.nn.sigmoid(cv)
    s_hi, s_lo = _split_bf16(s)
    w_hi, w_lo = _split_bf16(w_ref[...])
    o_ref[...] = _dot3(s_hi, s_lo, w_hi, w_lo) + b_ref[...]


def _adaln(cvec, ada_w, ada_b, interpret):
    depth = ada_w.shape[0]
    n_out = ada_w.shape[2]
    tn = D_MODEL
    return pl.pallas_call(
        _adaln_kernel,
        out_shape=jax.ShapeDtypeStruct((depth, SUBLANES, n_out), F32),
        grid=(depth, n_out // tn),
        in_specs=[
            pl.BlockSpec((SUBLANES, D_MODEL), lambda l, j: (0, 0)),
            pl.BlockSpec((None, D_MODEL, tn), lambda l, j: (l, 0, j)),
            pl.BlockSpec((None, 1, tn), lambda l, j: (l, 0, j)),
        ],
        out_specs=pl.BlockSpec((None, SUBLANES, tn), lambda l, j: (l, 0, j)),
        compiler_params=_cparams(("arbitrary", "arbitrary")),
        name="adaln_mod",
        interpret=interpret,
    )(cvec, ada_w, ada_b.reshape(depth, 1, n_out))


def _even_in_kernel(x_ref, mod_ref, g_ref, w_ref, o_ref, *, n_lat_tiles, tiles_per_batch, nb):
    i = pl.program_id(0)
    row = _mod_row(i, n_lat_tiles, tiles_per_batch, nb)
    sh = mod_ref[pl.ds(row, 1), 0:D_MODEL]
    sc = mod_ref[pl.ds(row, 1), D_MODEL:2 * D_MODEL]
    h = _rms(x_ref[...], g_ref[...]) * (1.0 + sc) + sh
    u = _dot(h.astype(BF16), w_ref[...])
    o_ref[:, 0:D_LRU] = u[:, 0:D_LRU]
    o_ref[:, D_LRU:2 * D_LRU] = jax.nn.gelu(u[:, D_LRU:2 * D_LRU])
    val = u[:, 2 * D_LRU:2 * D_LRU + D_CONV]
    gate = u[:, 2 * D_LRU + D_CONV:]
    o_ref[:, 2 * D_LRU:] = val * jax.nn.sigmoid(gate)


def _even_in(x, mod_l, g, w_in, dims, interpret):
    nb, s_len, c_len = dims
    t = x.shape[0]
    n_out = 2 * D_LRU + D_CONV
    kern = functools.partial(_even_in_kernel, n_lat_tiles=nb * s_len // TM,
                             tiles_per_batch=s_len // TM, nb=nb)
    return pl.pallas_call(
        kern,
        out_shape=jax.ShapeDtypeStruct((t, n_out), F32),
        grid=(t // TM,),
        in_specs=[
            pl.BlockSpec((TM, D_MODEL), lambda i: (i, 0)),
            pl.BlockSpec(mod_l.shape, lambda i: (0, 0)),
            pl.BlockSpec((1, D_MODEL), lambda i: (0, 0)),
            pl.BlockSpec(w_in.shape, lambda i: (0, 0)),
        ],
        out_specs=pl.BlockSpec((TM, n_out), lambda i: (i, 0)),
        compiler_params=_cparams(("arbitrary",)),
        name="even_in",
        interpret=interpret,
    )(x, mod_l, g.reshape(1, D_MODEL), w_in)


def _block_scan(a, b, reverse):
    n = a.shape[0]
    tmod = lax.broadcasted_iota(jnp.int32, a.shape, 0) & (SUBLANES - 1)
    d = 1
    while d < SUBLANES:
        if reverse:
            keep = tmod < SUBLANES - d
            a_n = pltpu.roll(a, n - d, axis=0)
            b_n = pltpu.roll(b, n - d, axis=0)
        else:
            keep = tmod >= d
            a_n = pltpu.roll(a, d, axis=0)
            b_n = pltpu.roll(b, d, axis=0)
        b = a * jnp.where(keep, b_n, 0.0) + b
        a = a * jnp.where(keep, a_n, 1.0)
        d *= 2
    return a, b


def _lru_coeffs(u, ub, wr, br, wi, bi, nsp):
    r = jax.nn.sigmoid(_dot(ub, wr) + br)
    i = jax.nn.sigmoid(_dot(ub, wi) + bi)
    log_a = (-LRU_C) * r * nsp
    a = jnp.exp(log_a)
    b = jnp.sqrt(1.0 - jnp.exp(2.0 * log_a)) * (i * u)
    return a, b


def _seq_kernel(xl_ref, xc_ref, gl_ref, gc_ref, vl_ref, vc_ref,
                cw_ref, cb_ref, wr_ref, br_ref, wi_ref, bi_ref, sp_ref, dw_ref, db_ref,
                ol_ref, oc_ref,
                af_ref, bf_ref, ar_ref, br_sc_ref, hf_ref, hr_ref, xp_ref, *, n_lru_tiles):
    c = pl.program_id(1)
    s_len = xl_ref.shape[0]
    c_len = xc_ref.shape[0]

    @pl.when(c < n_lru_tiles)
    def _lru():
        cw = cw_ref[...]
        cb = cb_ref[...]
        nsp = [jax.nn.softplus(-sp_ref[d]) for d in range(2)]

        def coeffs(x_ref, n):
            zpad = jnp.zeros((SUBLANES, CT), F32)
            xp_ref[0:SUBLANES] = zpad
            xp_ref[SUBLANES:SUBLANES + n] = x_ref[...]
            xp_ref[SUBLANES + n:2 * SUBLANES + n] = zpad
            ch = min(LRU_CHUNK, n)

            def body(j, _):
                t0 = pl.multiple_of(j * ch, ch)
                win = xp_ref[pl.ds(t0, ch + 2 * SUBLANES)]
                off = SUBLANES - LRU_CONV_W // 2
                u = cb
                for k in range(LRU_CONV_W):
                    u = u + cw[k:k + 1] * win[off + k:off + k + ch]
                ub = u.astype(BF16)
                a_f, b_f = _lru_coeffs(u, ub, wr_ref[0], br_ref[0], wi_ref[0], bi_ref[0], nsp[0])
                a_f, b_f = _block_scan(a_f, b_f, False)
                af_ref[pl.ds(t0, ch)] = a_f
                bf_ref[pl.ds(t0, ch)] = b_f
                a_r, b_r = _lru_coeffs(u, ub, wr_ref[1], br_ref[1], wi_ref[1], bi_ref[1], nsp[1])
                a_r, b_r = _block_scan(a_r, b_r, True)
                ar_ref[pl.ds(t0, ch)] = a_r
                br_sc_ref[pl.ds(t0, ch)] = b_r
                return 0

            lax.fori_loop(0, n // ch, body, 0)

        def carry_pass(n, h0f, h0r):
            nblk = n // SUBLANES

            def body(k, carry):
                hf, hr = carry
                kf = pl.multiple_of(k * SUBLANES, SUBLANES)
                kr = pl.multiple_of((nblk - 1 - k) * SUBLANES, SUBLANES)
                yf = bf_ref[pl.ds(kf, SUBLANES)] + af_ref[pl.ds(kf, SUBLANES)] * hf
                yr = br_sc_ref[pl.ds(kr, SUBLANES)] + ar_ref[pl.ds(kr, SUBLANES)] * hr
                hf_ref[pl.ds(kf, SUBLANES)] = yf
                hr_ref[pl.ds(kr, SUBLANES)] = yr
                return yf[SUBLANES - 1:SUBLANES], yr[0:1]

            return lax.fori_loop(0, nblk, body, (h0f, h0r))

        zero = jnp.zeros((1, CT), F32)
        coeffs(xc_ref, c_len)
        hcf, hcr = carry_pass(c_len, zero, zero)
        oc_ref[...] = (hf_ref[0:c_len] + hr_ref[0:c_len]) * gc_ref[...]
        coeffs(xl_ref, s_len)
        carry_pass(s_len, hcf, hcr)
        ol_ref[...] = (hf_ref[0:s_len] + hr_ref[0:s_len]) * gl_ref[...]

    @pl.when(c >= n_lru_tiles)
    def _conv():
        dw = dw_ref[...]
        db = db_ref[...]
        zpad = jnp.zeros((CONF_PAD, CT), F32)

        def conv(v_ref, o_ref, n):
            xp_ref[0:CONF_PAD] = zpad
            xp_ref[CONF_PAD:CONF_PAD + n] = v_ref[...]
            xp_ref[CONF_PAD + n:2 * CONF_PAD + n] = zpad

            def body(j, _):
                t0 = pl.multiple_of(j * CONV_CHUNK, CONV_CHUNK)
                win = xp_ref[pl.ds(t0, CONV_CHUNK + 2 * CONF_PAD)]
                acc = jnp.broadcast_to(db, (CONV_CHUNK, CT))
                off = CONF_PAD - CONF_WIDTH // 2
                for k in range(CONF_WIDTH):
                    acc = acc + dw[k:k + 1] * win[off + k:off + k + CONV_CHUNK]
                o_ref[pl.ds(t0, CONV_CHUNK)] = acc
                return 0

            lax.fori_loop(0, n // CONV_CHUNK, body, 0)

        conv(vc_ref, oc_ref, c_len)
        conv(vl_ref, ol_ref, s_len)


def _seq_mix(u_all, p, dims, interpret):
    nb, s_len, c_len = dims
    n_lru = D_LRU // CT
    n_cv = D_CONV // CT
    lat_blocks = nb * s_len // c_len

    def lru_c(c):
        return jnp.minimum(c, n_lru - 1)

    def cv_c(c):
        return jnp.maximum(c - n_lru, 0)

    in_specs = [
        pl.BlockSpec((s_len, CT), lambda b, c: (b, lru_c(c))),
        pl.BlockSpec((c_len, CT), lambda b, c: (lat_blocks + b, lru_c(c))),
        pl.BlockSpec((s_len, CT), lambda b, c: (b, n_lru + lru_c(c))),
        pl.BlockSpec((c_len, CT), lambda b, c: (lat_blocks + b, n_lru + lru_c(c))),
        pl.BlockSpec((s_len, CT), lambda b, c: (b, 2 * n_lru + cv_c(c))),
        pl.BlockSpec((c_len, CT), lambda b, c: (lat_blocks + b, 2 * n_lru + cv_c(c))),
        pl.BlockSpec((LRU_CONV_W, CT), lambda b, c: (0, lru_c(c))),
        pl.BlockSpec((1, CT), lambda b, c: (0, lru_c(c))),
        pl.BlockSpec((2, None, CT, CT), lambda b, c: (0, lru_c(c), 0, 0)),
        pl.BlockSpec((2, 1, CT), lambda b, c: (0, 0, lru_c(c))),
        pl.BlockSpec((2, None, CT, CT), lambda b, c: (0, lru_c(c), 0, 0)),
        pl.BlockSpec((2, 1, CT), lambda b, c: (0, 0, lru_c(c))),
        pl.BlockSpec((2, 1, CT), lambda b, c: (0, 0, lru_c(c))),
        pl.BlockSpec((CONF_WIDTH, CT), lambda b, c: (0, cv_c(c))),
        pl.BlockSpec((1, CT), lambda b, c: (0, cv_c(c))),
    ]
    out_specs = [
        pl.BlockSpec((s_len, CT), lambda b, c: (b, c)),
        pl.BlockSpec((c_len, CT), lambda b, c: (b, c)),
    ]
    scratch = [pltpu.VMEM((s_len, CT), F32) for _ in range(6)]
    scratch.append(pltpu.VMEM((s_len + 2 * CONF_PAD, CT), F32))
    return pl.pallas_call(
        functools.partial(_seq_kernel, n_lru_tiles=n_lru),
        out_shape=[jax.ShapeDtypeStruct((nb * s_len, D_LRU + D_CONV), F32),
                   jax.ShapeDtypeStruct((nb * c_len, D_LRU + D_CONV), F32)],
        grid=(nb, n_lru + n_cv),
        in_specs=in_specs,
        out_specs=out_specs,
        scratch_shapes=scratch,
        compiler_params=_cparams(("arbitrary", "arbitrary")),
        name="seq_mix",
        interpret=interpret,
    )(u_all, u_all, u_all, u_all, u_all, u_all,
      p["conv_w"], p["conv_b"], p["w_r"], p["b_r"], p["w_i"], p["b_i"], p["lam"], p["dw_w"], p["dw_b"])


def _swap32(x):
    lane = lax.broadcasted_iota(jnp.int32, x.shape, 1)
    return jnp.where((lane & 32) == 0, pltpu.roll(x, HEAD_DIM - 32, axis=1), pltpu.roll(x, 32, axis=1))


def _odd_in_kernel(x_ref, mod_ref, g_ref, w_ref, qg_ref, kg_ref, cos_ref, sin_ref, o_ref,
                   *, n_lat_tiles, tiles_per_batch, nb):
    i = pl.program_id(0)
    row = _mod_row(i, n_lat_tiles, tiles_per_batch, nb)
    sh = mod_ref[pl.ds(row, 1), 0:D_MODEL]
    sc = mod_ref[pl.ds(row, 1), D_MODEL:2 * D_MODEL]
    h = _rms(x_ref[...], g_ref[...]) * (1.0 + sc) + sh
    u = _dot(h.astype(BF16), w_ref[...])
    is_lat = i < n_lat_tiles
    cos = jnp.where(is_lat, cos_ref[...], 1.0)
    sin = jnp.where(is_lat, sin_ref[...], 0.0)
    q_scale = 1.0 / math.sqrt(HEAD_DIM)
    for hh in range(N_Q_HEADS + N_KV_HEADS):
        xh = u[:, hh * HEAD_DIM:(hh + 1) * HEAD_DIM]
        gain = qg_ref[...] if hh < N_Q_HEADS else kg_ref[...]
        xn = _rms(xh, gain)
        y = xn * cos + _swap32(xn) * sin
        if hh < N_Q_HEADS:
            y = y * q_scale
        o_ref[:, hh * HEAD_DIM:(hh + 1) * HEAD_DIM] = y.astype(BF16)
    v0 = (N_Q_HEADS + N_KV_HEADS) * HEAD_DIM
    o_ref[:, v0:] = u[:, v0:].astype(BF16)


def _odd_in(x, mod_l, g, w_qkv, q_g, k_g, rope_cos, rope_sin, dims, interpret):
    nb, s_len, c_len = dims
    t = x.shape[0]
    n_out = w_qkv.shape[1]
    tpb = s_len // TM
    n_lat = nb * tpb
    kern = functools.partial(_odd_in_kernel, n_lat_tiles=n_lat, tiles_per_batch=tpb, nb=nb)

    def pos_block(i):
        return jnp.where(i < n_lat, i % tpb, 0)

    return pl.pallas_call(
        kern,
        out_shape=jax.ShapeDtypeStruct((t, n_out), BF16),
        grid=(t // TM,),
        in_specs=[
            pl.BlockSpec((TM, D_MODEL), lambda i: (i, 0)),
            pl.BlockSpec(mod_l.shape, lambda i: (0, 0)),
            pl.BlockSpec((1, D_MODEL), lambda i: (0, 0)),
            pl.BlockSpec(w_qkv.shape, lambda i: (0, 0)),
            pl.BlockSpec((1, HEAD_DIM), lambda i: (0, 0)),
            pl.BlockSpec((1, HEAD_DIM), lambda i: (0, 0)),
            pl.BlockSpec((TM, HEAD_DIM), lambda i: (pos_block(i), 0)),
            pl.BlockSpec((TM, HEAD_DIM), lambda i: (pos_block(i), 0)),
        ],
        out_specs=pl.BlockSpec((TM, n_out), lambda i: (i, 0)),
        compiler_params=_cparams(("arbitrary",)),
        name="odd_in",
        interpret=interpret,
    )(x, mod_l, g.reshape(1, D_MODEL), w_qkv, q_g.reshape(1, HEAD_DIM), k_g.reshape(1, HEAD_DIM),
      rope_cos, rope_sin)


def _attn_kernel(*refs, n_seg):
    q_ref = refs[0]
    k_refs = refs[1:1 + n_seg]
    v_refs = refs[1 + n_seg:1 + 2 * n_seg]
    o_ref = refs[1 + 2 * n_seg]
    tq = q_ref.shape[0]
    chunks = []
    for k_ref, v_ref in zip(k_refs, v_refs):
        n = k_ref.shape[0]
        step = min(TK, n)
        for s0 in range(0, n, step):
            chunks.append((k_ref, v_ref, s0, step))
    for g in range(GQA_GROUP):
        q = q_ref[:, g * HEAD_DIM:(g + 1) * HEAD_DIM]
        m = jnp.full((tq, 1), -jnp.inf, F32)
        l = jnp.zeros((tq, 1), F32)
        acc = jnp.zeros((tq, HEAD_DIM), F32)
        for k_ref, v_ref, s0, step in chunks:
            k = k_ref[s0:s0 + step, :]
            v = v_ref[s0:s0 + step, :]
            s = lax.dot_general(q, k, (((1,), (1,)), ((), ())), preferred_element_type=F32)
            m_new = jnp.maximum(m, jnp.max(s, axis=-1, keepdims=True))
            alpha = jnp.exp(m - m_new)
            p = jnp.exp(s - m_new)
            l = alpha * l + jnp.sum(p, axis=-1, keepdims=True)
            acc = alpha * acc + _dot(p.astype(BF16), v)
            m = m_new
        o_ref[:, g * HEAD_DIM:(g + 1) * HEAD_DIM] = (acc / l).astype(BF16)


def _attention(qkv, dims, latent, interpret):
    nb, s_len, c_len = dims
    k_col = N_Q_HEADS
    v_col = N_Q_HEADS + N_KV_HEADS
    lat_blocks = nb * s_len // c_len
    ctx_k = pl.BlockSpec((c_len, HEAD_DIM), lambda b, h, qi: (lat_blocks + b, k_col + h))
    ctx_v = pl.BlockSpec((c_len, HEAD_DIM), lambda b, h, qi: (lat_blocks + b, v_col + h))
    if latent:
        tq = TQ
        nq = s_len // tq
        q_spec = pl.BlockSpec((tq, GQA_GROUP * HEAD_DIM), lambda b, h, qi: (b * nq + qi, h))
        k_specs = [pl.BlockSpec((s_len, HEAD_DIM), lambda b, h, qi: (b, k_col + h)), ctx_k]
        v_specs = [pl.BlockSpec((s_len, HEAD_DIM), lambda b, h, qi: (b, v_col + h)), ctx_v]
        out_rows = nb * s_len
        o_spec = pl.BlockSpec((tq, GQA_GROUP * HEAD_DIM), lambda b, h, qi: (b * nq + qi, h))
    else:
        tq = c_len
        nq = 1
        q_spec = pl.BlockSpec((tq, GQA_GROUP * HEAD_DIM), lambda b, h, qi: (lat_blocks + b, h))
        k_specs = [ctx_k]
        v_specs = [ctx_v]
        out_rows = nb * c_len
        o_spec = pl.BlockSpec((tq, GQA_GROUP * HEAD_DIM), lambda b, h, qi: (b, h))
    n_seg = len(k_specs)
    return pl.pallas_call(
        functools.partial(_attn_kernel, n_seg=n_seg),
        out_shape=jax.ShapeDtypeStruct((out_rows, N_Q_HEADS * HEAD_DIM), BF16),
        grid=(nb, N_KV_HEADS, nq),
        in_specs=[q_spec] + k_specs + v_specs,
        out_specs=o_spec,
        compiler_params=_cparams(("arbitrary", "arbitrary", "arbitrary")),
        name="attn_lat" if latent else "attn_ctx",
        interpret=interpret,
    )(*([qkv] * (1 + 2 * n_seg)))


def _route(logits_t, bias):
    score = [jax.nn.sigmoid(logits_t[e:e + 1]) for e in range(N_EXPERTS)]
    sel = [score[e] + bias[e] for e in range(N_EXPERTS)]
    grp = []
    for g in range(N_GROUPS):
        a, b, c, d = sel[4 * g:4 * g + 4]
        hi1, lo1 = jnp.maximum(a, b), jnp.minimum(a, b)
        hi2, lo2 = jnp.maximum(c, d), jnp.minimum(c, d)
        top1 = jnp.maximum(hi1, hi2)
        top2 = jnp.maximum(jnp.minimum(hi1, hi2), jnp.maximum(lo1, lo2))
        grp.append(top1 + top2)
    best = jnp.zeros_like(grp[0], dtype=jnp.int32)
    best_s = grp[0]
    for g in range(1, N_GROUPS):
        better = grp[g] > best_s
        best = jnp.where(better, g, best)
        best_s = jnp.where(better, grp[g], best_s)
    chosen = []
    for e in range(N_EXPERTS):
        g = e // EXPERTS_PER_GROUP
        beaten = jnp.zeros_like(best)
        for k in range(4 * g, 4 * g + 4):
            if k == e:
                continue
            beats = (sel[k] >= sel[e]) if k < e else (sel[k] > sel[e])
            beaten = beaten + jnp.where(beats, 1, 0)
        chosen.append((beaten < 2) & (best == g))
    wsum = jnp.zeros_like(score[0])
    for e in range(N_EXPERTS):
        wsum = wsum + jnp.where(chosen[e], score[e], 0.0)
    gates = [jnp.where(chosen[e], score[e] / wsum, 0.0) for e in range(N_EXPERTS)]
    in_group = []
    for j in range(EXPERTS_PER_GROUP):
        gj = jnp.zeros_like(wsum)
        for g in range(N_GROUPS):
            gj = jnp.where(best == g, gates[EXPERTS_PER_GROUP * g + j], gj)
        in_group.append(gj)
    return best, in_group


def _mix_out_kernel(yl_ref, yc_ref, x_ref, mod_ref, w_ref, lng_ref, lnb_ref, nfg_ref, rwh_ref, rwl_ref, rb_ref,
                    xo_ref, hrow_ref, pos_ref, cnt_ref, base_ref,
                    *, even, n_lat_tiles, tiles_per_batch, nb, cap):
    i = pl.program_id(0)

    @pl.when(i == 0)
    def _init():
        base_ref[...] = jnp.zeros_like(base_ref)

    row = _mod_row(i, n_lat_tiles, tiles_per_batch, nb)
    g1 = mod_ref[pl.ds(row, 1), 2 * D_MODEL:3 * D_MODEL]
    sh2 = mod_ref[pl.ds(row, 1), 3 * D_MODEL:4 * D_MODEL]
    sc2 = mod_ref[pl.ds(row, 1), 4 * D_MODEL:5 * D_MODEL]
    is_lat = i < n_lat_tiles
    y = jnp.where(is_lat, yl_ref[...], yc_ref[...])
    if even:
        ya = y[:, 0:D_LRU]
        vc = y[:, D_LRU:]
        mu = jnp.mean(vc, axis=-1, keepdims=True)
        xc = vc - mu
        var = jnp.mean(xc * xc, axis=-1, keepdims=True)
        ln = xc * lax.rsqrt(var + NORM_EPS) * lng_ref[...] + lnb_ref[...]
        yb = ln * jax.nn.sigmoid(ln)
        out = _dot(ya.astype(BF16), w_ref[0:D_LRU, :]) + _dot(yb.astype(BF16), w_ref[D_LRU:, :])
    else:
        out = _dot(y, w_ref[...])
    x_new = x_ref[...] + g1 * out
    xo_ref[...] = x_new
    h2 = _rms(x_new, nfg_ref[...]) * (1.0 + sc2) + sh2
    hrow_ref[:, 0:D_MODEL] = h2
    h_hi, h_lo = _split_bf16(h2)
    logits = _dot3(h_hi, h_lo, rwh_ref[...], rwl_ref[...])
    logits_t = jnp.transpose(logits)[0:N_EXPERTS]
    best, in_group = _route(logits_t, [rb_ref[e] for e in range(N_EXPERTS)])
    tm = logits.shape[0]

    sub = lax.broadcasted_iota(jnp.int32, (SUBLANES, tm), 0)
    onehot = jnp.where(sub == best, 1.0, 0.0)
    earlier = (lax.broadcasted_iota(jnp.int32, (tm, tm), 0) < lax.broadcasted_iota(jnp.int32, (tm, tm), 1))
    before = _dot(onehot.astype(BF16), jnp.where(earlier, 1.0, 0.0).astype(BF16))
    rank = jnp.sum(onehot * (before + base_ref[:, 0:1]), axis=0, keepdims=True)
    pos_ref[...] = best * cap + rank.astype(jnp.int32)
    base_ref[...] = base_ref[...] + jnp.sum(onehot, axis=1, keepdims=True)
    cnt_ref[...] = base_ref[...]

    extras_t = jnp.zeros((SUBLANES, tm), F32)
    for j in range(EXPERTS_PER_GROUP):
        extras_t = jnp.where(sub == j, in_group[j], extras_t)
    extras_t = jnp.concatenate([extras_t, jnp.zeros((LANES - SUBLANES, tm), F32)], axis=0)
    hrow_ref[:, D_MODEL:] = jnp.transpose(extras_t)


def _mix_out(y_lat, y_ctx, x, mod_l, w_out, ln_g, ln_b, nf_g, rw_hi, rw_lo, router_bias, dims, n_tiles, cap, even,
             interpret):
    nb, s_len, c_len = dims
    tpb = s_len // TM
    n_lat = nb * tpb
    n_ctx = max(nb * c_len // TM, 1)
    rows = n_tiles * TM
    kern = functools.partial(_mix_out_kernel, even=even, n_lat_tiles=n_lat, tiles_per_batch=tpb, nb=nb, cap=cap)
    vec = lambda n: pl.BlockSpec((1, n), lambda i: (0, 0))
    return pl.pallas_call(
        kern,
        out_shape=[jax.ShapeDtypeStruct((rows, D_MODEL), F32),
                   jax.ShapeDtypeStruct((rows, ROW_W), F32),
                   jax.ShapeDtypeStruct((n_tiles, 1, TM), jnp.int32),
                   jax.ShapeDtypeStruct((SUBLANES, LANES), F32)],
        grid=(n_tiles,),
        in_specs=[
            pl.BlockSpec((TM, D_MODEL), lambda i: (jnp.minimum(i, n_lat - 1), 0)),
            pl.BlockSpec((TM, D_MODEL), lambda i: (jnp.clip(i - n_lat, 0, n_ctx - 1), 0)),
            pl.BlockSpec((TM, D_MODEL), lambda i: (i, 0)),
            pl.BlockSpec(mod_l.shape, lambda i: (0, 0)),
            pl.BlockSpec(w_out.shape, lambda i: (0, 0)),
            vec(D_CONV), vec(D_CONV), vec(D_MODEL),
            pl.BlockSpec(rw_hi.shape, lambda i: (0, 0)),
            pl.BlockSpec(rw_lo.shape, lambda i: (0, 0)),
            pl.BlockSpec(memory_space=pltpu.SMEM),
        ],
        out_specs=[
            pl.BlockSpec((TM, D_MODEL), lambda i: (i, 0)),
            pl.BlockSpec((TM, ROW_W), lambda i: (i, 0)),
            pl.BlockSpec((None, 1, TM), lambda i: (i, 0, 0)),
            pl.BlockSpec((SUBLANES, LANES), lambda i: (0, 0)),
        ],
        scratch_shapes=[pltpu.VMEM((SUBLANES, LANES), F32)],
        compiler_params=_cparams(("arbitrary",)),
        name="mix_out_even" if even else "mix_out_odd",
        interpret=interpret,
    )(y_lat, y_ctx, x, mod_l, w_out, ln_g.reshape(1, D_CONV), ln_b.reshape(1, D_CONV),
      nf_g.reshape(1, D_MODEL), rw_hi, rw_lo, router_bias)


def _row_copy(src_ref, src_row, dst_ref, dst_row, sem):
    return pltpu.make_async_copy(src_ref.at[pl.ds(src_row, 1)], dst_ref.at[pl.ds(dst_row, 1)], sem)


def _dispatch_kernel(pos_ref, fill_ref, nact_ref, h_ref, xs_ref, zero_ref, row_sem, fill_sem, *, n_slots):
    i = pl.program_id(0)
    tm = h_ref.shape[0]

    @pl.when(i == 0)
    def _fill():
        zero_ref[...] = jnp.zeros_like(zero_ref)
        for g in range(N_GROUPS):
            start = pl.multiple_of(fill_ref[g], SUBLANES)
            pltpu.make_async_copy(zero_ref, xs_ref.at[pl.ds(start, TMS)], fill_sem.at[g]).start()
        for g in range(N_GROUPS):
            pltpu.make_async_copy(zero_ref, xs_ref.at[pl.ds(0, TMS)], fill_sem.at[g]).wait()
        for k in range(N_GROUPS):
            slot = n_slots - 1 - k

            @pl.when(slot >= nact_ref[0])
            def _tail():
                tail = pltpu.make_async_copy(zero_ref, xs_ref.at[pl.ds(slot * TMS, TMS)], fill_sem.at[k])
                tail.start()
                tail.wait()

    def issue(r, _):
        _row_copy(h_ref, r, xs_ref, pos_ref[i * tm + r], row_sem).start()
        return 0

    lax.fori_loop(0, tm, issue, 0)
    pltpu.make_async_copy(h_ref, xs_ref.at[pl.ds(0, tm)], row_sem).wait()


def _dispatch(hrow, pos_flat, fill_start, n_active, n_slots, interpret):
    rows = hrow.shape[0]
    return pl.pallas_call(
        functools.partial(_dispatch_kernel, n_slots=n_slots),
        out_shape=jax.ShapeDtypeStruct((n_slots * TMS, ROW_W), F32),
        grid_spec=pltpu.PrefetchScalarGridSpec(
            num_scalar_prefetch=3,
            grid=(rows // TM,),
            in_specs=[pl.BlockSpec((TM, ROW_W), lambda i, pos, fill, nact: (i, 0))],
            out_specs=pl.BlockSpec(memory_space=pl.ANY),
            scratch_shapes=[pltpu.VMEM((TMS, ROW_W), F32), pltpu.SemaphoreType.DMA,
                            pltpu.SemaphoreType.DMA((N_GROUPS,))],
        ),
        compiler_params=_cparams(("arbitrary",)),
        name="moe_dispatch",
        interpret=interpret,
    )(pos_flat, fill_start, n_active, hrow)


def _experts_kernel(grp_ref, nact_ref, xs_ref, wg_ref, wu_ref, wd_ref, ys_ref, hb_ref, acc_ref):
    i = pl.program_id(0)
    e = pl.program_id(1)

    @pl.when(i < nact_ref[0])
    def _active():
        @pl.when(e == 0)
        def _init():
            hb_ref[...] = xs_ref[:, 0:D_MODEL].astype(BF16)
            acc_ref[...] = jnp.zeros_like(acc_ref)

        h = hb_ref[...]
        ext = xs_ref[:, D_MODEL:]
        lane = lax.broadcasted_iota(jnp.int32, ext.shape, 1)
        gate = jnp.sum(jnp.where(lane == e, ext, 0.0), axis=-1, keepdims=True)
        a = _dot(h, wg_ref[...])
        u = _dot(h, wu_ref[...])
        hid = (a * jax.nn.sigmoid(a)) * u
        acc_ref[...] += gate * _dot(hid.astype(BF16), wd_ref[...])

        @pl.when(e == EXPERTS_PER_GROUP - 1)
        def _fin():
            ys_ref[...] = acc_ref[...]

    @pl.when((i >= nact_ref[0]) & (e == 0))
    def _unused():
        ys_ref[...] = jnp.zeros_like(ys_ref)


def _experts(xs, tile_grp, n_active, wg, wu, wd, interpret):
    n_slots = tile_grp.shape[0]
    w_idx = lambda i, e, grp, nact: (grp[i] * EXPERTS_PER_GROUP + e, 0, 0)
    return pl.pallas_call(
        _experts_kernel,
        out_shape=jax.ShapeDtypeStruct((n_slots * TMS, D_MODEL), F32),
        grid_spec=pltpu.PrefetchScalarGridSpec(
            num_scalar_prefetch=2,
            grid=(n_slots, EXPERTS_PER_GROUP),
            in_specs=[
                pl.BlockSpec((TMS, ROW_W), lambda i, e, grp, nact: (i, 0)),
                pl.BlockSpec((None, D_MODEL, D_EXPERT), w_idx),
                pl.BlockSpec((None, D_MODEL, D_EXPERT), w_idx),
                pl.BlockSpec((None, D_EXPERT, D_MODEL), w_idx),
            ],
            out_specs=pl.BlockSpec((TMS, D_MODEL), lambda i, e, grp, nact: (i, 0)),
            scratch_shapes=[pltpu.VMEM((TMS, D_MODEL), BF16), pltpu.VMEM((TMS, D_MODEL), F32)],
        ),
        compiler_params=_cparams(("arbitrary", "arbitrary")),
        name="moe_experts",
        interpret=interpret,
    )(tile_grp, n_active, xs, wg, wu, wd)


def _combine_kernel(pos_ref, ys_ref, x_ref, mod_ref, fg_ref, o_ref, ybuf_ref, sem,
                    *, final, n_lat_tiles, tiles_per_batch, nb):
    i = pl.program_id(0)
    tm = x_ref.shape[0]

    def issue(r, _):
        _row_copy(ys_ref, pos_ref[i * tm + r], ybuf_ref, r, sem).start()
        return 0

    lax.fori_loop(0, tm, issue, 0)
    pltpu.make_async_copy(ys_ref.at[pl.ds(0, tm)], ybuf_ref, sem).wait()
    row = _mod_row(i, n_lat_tiles, tiles_per_batch, nb)
    g2 = mod_ref[pl.ds(row, 1), 5 * D_MODEL:6 * D_MODEL]
    x_new = x_ref[...] + g2 * ybuf_ref[...]
    if final:
        x_new = _rms(x_new, fg_ref[...])
    o_ref[...] = x_new


def _combine(ys, pos_flat, x, mod_l, final_g, dims, final, interpret):
    nb, s_len, c_len = dims
    rows = x.shape[0]
    tpb = s_len // TM
    kern = functools.partial(_combine_kernel, final=final, n_lat_tiles=nb * tpb, tiles_per_batch=tpb, nb=nb)
    return pl.pallas_call(
        kern,
        out_shape=jax.ShapeDtypeStruct((rows, D_MODEL), F32),
        grid_spec=pltpu.PrefetchScalarGridSpec(
            num_scalar_prefetch=1,
            grid=(rows // TM,),
            in_specs=[
                pl.BlockSpec(memory_space=pl.ANY),
                pl.BlockSpec((TM, D_MODEL), lambda i, pos: (i, 0)),
                pl.BlockSpec(mod_l.shape, lambda i, pos: (0, 0)),
                pl.BlockSpec((1, D_MODEL), lambda i, pos: (0, 0)),
            ],
            out_specs=pl.BlockSpec((TM, D_MODEL), lambda i, pos: (i, 0)),
            scratch_shapes=[pltpu.VMEM((TM, D_MODEL), F32), pltpu.SemaphoreType.DMA],
        ),
        compiler_params=_cparams(("arbitrary",)),
        name="moe_combine_final" if final else "moe_combine",
        interpret=interpret,
    )(pos_flat, ys, x, mod_l, final_g.reshape(1, D_MODEL))


def _sorted_layout(pos, counts, cap, n_slots):
    cnt = counts[0:N_GROUPS, 0].astype(jnp.int32)
    tiles = (cnt + TMS - 1) // TMS
    ends = jnp.cumsum(tiles)
    starts = (ends - tiles) * TMS
    n_active = ends[-1]
    pos_flat = pos.reshape(-1)
    pos_sorted = jnp.take(starts, pos_flat // cap) + pos_flat % cap
    fill_start = ((starts + cnt) // SUBLANES) * SUBLANES
    slot = jnp.minimum(jnp.arange(n_slots, dtype=jnp.int32), n_active - 1)
    tile_grp = jnp.sum((slot[:, None] >= ends[None, :]).astype(jnp.int32), axis=1)
    i32 = lambda a: a.astype(jnp.int32)
    return i32(pos_sorted), i32(fill_start), i32(tile_grp), i32(n_active.reshape(1))


def _moe(x_mid, hrow, pos, counts, mod_l, wg, wu, wd, final_g, dims, cap, final, interpret):
    rows = x_mid.shape[0]
    n_slots = rows // TMS + N_GROUPS
    pos_sorted, fill_start, tile_grp, n_active = _sorted_layout(pos, counts, cap, n_slots)
    xs = _dispatch(hrow, pos_sorted, fill_start, n_active, n_slots, interpret)
    ys = _experts(xs, tile_grp, n_active, wg, wu, wd, interpret)
    return _combine(ys, pos_sorted, x_mid, mod_l, final_g, dims, final, interpret)


def _rope_tables(s_len):
    rows = s_len // GRID_W
    quarter = HEAD_DIM // 4
    r, col = jnp.meshgrid(jnp.arange(rows, dtype=F32), jnp.arange(GRID_W, dtype=F32), indexing="ij")
    inv_freq = ROPE_THETA ** (-jnp.arange(0, HEAD_DIM // 2, 2, dtype=F32) / (HEAD_DIM // 2))
    ang_r = r.reshape(-1)[:, None] * inv_freq
    ang_c = col.reshape(-1)[:, None] * inv_freq
    cos = jnp.concatenate([jnp.cos(ang_r)] * 2 + [jnp.cos(ang_c)] * 2, axis=-1)
    sin = jnp.concatenate([-jnp.sin(ang_r), jnp.sin(ang_r), -jnp.sin(ang_c), jnp.sin(ang_c)], axis=-1)
    assert cos.shape == (s_len, 4 * quarter)
    return cos, sin


def _block_diag_tiles(w):
    per = CT // LRU_HEAD_DIM
    n_tiles = D_LRU // CT
    w = w.reshape(2, n_tiles, per, LRU_HEAD_DIM, LRU_HEAD_DIM)
    eye = jnp.eye(per, dtype=w.dtype)
    full = jnp.einsum("dtpij,pq->dtpiqj", w, eye)
    return full.reshape(2, n_tiles, CT, CT).astype(BF16)


def _forward(x, c, ctx, c_ctx, ada_w, ada_b, norm_mix_g, norm_ffn_g, ev_w_in, ev_w_out,
             lru_conv_w, lru_conv_b, lru_w_r, lru_b_r, lru_w_i, lru_b_i, lru_lambda,
             cf_dw_w, cf_dw_b, cf_ln_g, cf_ln_b, at_w_qkv, at_w_out, at_q_norm_g, at_k_norm_g,
             router_w, router_bias, exp_w_gate, exp_w_up, exp_w_down, final_norm_g, interpret=False):
    nb, s_len, d = x.shape
    c_len = ctx.shape[1]
    depth = ada_w.shape[0]
    dims = (nb, s_len, c_len)
    assert d == D_MODEL and nb < SUBLANES
    assert s_len % TM == 0 and (nb * c_len) % TM == 0 and s_len % c_len == 0 and s_len % TQ == 0
    n_lat_tiles = nb * s_len // TM
    n_all_tiles = n_lat_tiles + nb * c_len // TM

    cvec = jnp.zeros((SUBLANES, d), F32).at[:nb].set(c).at[nb].set(c_ctx)
    mod = _adaln(cvec, ada_w, ada_b, interpret)
    xf = jnp.concatenate([x.reshape(nb * s_len, d), ctx.reshape(nb * c_len, d)], axis=0)

    rope_cos, rope_sin = _rope_tables(s_len)
    rw = jnp.zeros((d, LANES), F32).at[:, :N_EXPERTS].set(router_w)
    rw_hi, rw_lo = _split_bf16(rw)
    rb = router_bias.astype(F32)

    for l in range(depth):
        last = l == depth - 1
        n_tiles = n_lat_tiles if last else n_all_tiles
        mod_l = mod[l]
        if l % 2 == 0:
            e = l // 2
            u_all = _even_in(xf, mod_l, norm_mix_g[l], ev_w_in[e].astype(BF16), dims, interpret)
            p = {
                "conv_w": lru_conv_w[e], "conv_b": lru_conv_b[e].reshape(1, D_LRU),
                "w_r": _block_diag_tiles(lru_w_r[e]), "b_r": lru_b_r[e].reshape(2, 1, D_LRU),
                "w_i": _block_diag_tiles(lru_w_i[e]), "b_i": lru_b_i[e].reshape(2, 1, D_LRU),
                "lam": lru_lambda[e].reshape(2, 1, D_LRU),
                "dw_w": cf_dw_w[e], "dw_b": cf_dw_b[e].reshape(1, D_CONV),
            }
            y_lat, y_ctx = _seq_mix(u_all, p, dims, interpret)
            w_out = ev_w_out[e].astype(BF16)
            ln_g, ln_b = cf_ln_g[e], cf_ln_b[e]
        else:
            o = l // 2
            qkv = _odd_in(xf, mod_l, norm_mix_g[l], at_w_qkv[o].astype(BF16), at_q_norm_g[o], at_k_norm_g[o],
                          rope_cos, rope_sin, dims, interpret)
            y_lat = _attention(qkv, dims, True, interpret)
            y_ctx = y_lat if last else _attention(qkv, dims, False, interpret)
            w_out = at_w_out[o].astype(BF16)
            ln_g, ln_b = cf_ln_g[0], cf_ln_b[0]
        cap = n_tiles * TM + TMS
        x_mid, hrow, pos, counts = _mix_out(y_lat, y_ctx, xf, mod_l, w_out, ln_g, ln_b, norm_ffn_g[l], rw_hi, rw_lo,
                                            rb, dims, n_tiles, cap, l % 2 == 0, interpret)
        xf = _moe(x_mid, hrow, pos, counts, mod_l, exp_w_gate[l].astype(BF16), exp_w_up[l].astype(BF16),
                  exp_w_down[l].astype(BF16), final_norm_g, dims, cap, last, interpret)
    return xf.reshape(nb, s_len, d)


def kernel(x, c, ctx, c_ctx, ada_w, ada_b, norm_mix_g, norm_ffn_g, ev_w_in, ev_w_out, lru_conv_w, lru_conv_b,
           lru_w_r, lru_b_r, lru_w_i, lru_b_i, lru_lambda, cf_dw_w, cf_dw_b, cf_ln_g, cf_ln_b, at_w_qkv, at_w_out,
           at_q_norm_g, at_k_norm_g, router_w, router_bias, exp_w_gate, exp_w_up, exp_w_down, final_norm_g):
    return _forward(x, c, ctx, c_ctx, ada_w, ada_b, norm_mix_g, norm_ffn_g, ev_w_in, ev_w_out, lru_conv_w,
                    lru_conv_b, lru_w_r, lru_b_r, lru_w_i, lru_b_i, lru_lambda, cf_dw_w, cf_dw_b, cf_ln_g,
                    cf_ln_b, at_w_qkv, at_w_out, at_q_norm_g, at_k_norm_g, router_w, router_bias, exp_w_gate,
                    exp_w_up, exp_w_down, final_norm_g)
```

```python
import functools
import math

import jax
import jax.numpy as jnp
from jax import lax
from jax.experimental import pallas as pl
from jax.experimental.pallas import tpu as pltpu

F32 = jnp.float32
BF16 = jnp.bfloat16

D_MODEL = 1024
N_MOD = 6
NORM_EPS = 1e-6
GRID_W = 64

D_LRU = 512
LRU_HEADS = 8
LRU_HEAD_DIM = D_LRU // LRU_HEADS
LRU_CONV_W = 4
LRU_C = 8.0
D_CONV = 512
CONF_WIDTH = 31
CONF_PAD = 16

HEAD_DIM = 128
N_Q_HEADS = 8
N_KV_HEADS = 2
GQA_GROUP = N_Q_HEADS // N_KV_HEADS
ROPE_THETA = 10000.0

N_EXPERTS = 16
N_GROUPS = 4
EXPERTS_PER_GROUP = 4
D_EXPERT = 512

LANES = 128
SUBLANES = 8
TM = 512
TMS = 256
PAIRS_PER_GROUP = 6
N_CLASSES = N_GROUPS * PAIRS_PER_GROUP
CLASS_ROWS = 32
ISSUE_UNROLL = 8
TD = 1024
ROW_W = D_MODEL + LANES
CT = 128
TQ = 512
TK = 1024
CONV_CHUNK = 128
LRU_CHUNK = 256
VMEM_LIMIT = 56 * 1024 * 1024


def _cparams(sem):
    return pltpu.CompilerParams(dimension_semantics=sem, vmem_limit_bytes=VMEM_LIMIT)


def _rms(x, g):
    return x * lax.rsqrt(jnp.mean(x * x, axis=-1, keepdims=True) + NORM_EPS) * g


def _split_bf16(a):
    hi = a.astype(BF16)
    lo = (a - hi.astype(F32)).astype(BF16)
    return hi, lo


def _dot(a, b):
    return jnp.dot(a, b, preferred_element_type=F32)


def _dot3(a_hi, a_lo, b_hi, b_lo):
    return _dot(a_hi, b_hi) + (_dot(a_lo, b_hi) + _dot(a_hi, b_lo))


def _mod_row(i, n_lat_tiles, tiles_per_batch, nb):
    return jnp.where(i < n_lat_tiles, i // tiles_per_batch, nb)


def _adaln_kernel(c_ref, w_ref, b_ref, o_ref):
    cv = c_ref[...]
    s = cv * jax.nn.sigmoid(cv)
    s_hi, s_lo = _split_bf16(s)
    w_hi, w_lo = _split_bf16(w_ref[...])
    o_ref[...] = _dot3(s_hi, s_lo, w_hi, w_lo) + b_ref[...]


def _adaln(cvec, ada_w, ada_b, interpret):
    depth = ada_w.shape[0]
    n_out = ada_w.shape[2]
    tn = D_MODEL
    return pl.pallas_call(
        _adaln_kernel,
        out_shape=jax.ShapeDtypeStruct((depth, SUBLANES, n_out), F32),
        grid=(depth, n_out // tn),
        in_specs=[
            pl.BlockSpec((SUBLANES, D_MODEL), lambda l, j: (0, 0)),
            pl.BlockSpec((None, D_MODEL, tn), lambda l, j: (l, 0, j)),
            pl.BlockSpec((None, 1, tn), lambda l, j: (l, 0, j)),
        ],
        out_specs=pl.BlockSpec((None, SUBLANES, tn), lambda l, j: (l, 0, j)),
        compiler_params=_cparams(("arbitrary", "arbitrary")),
        name="adaln_mod",
        interpret=interpret,
    )(cvec, ada_w, ada_b.reshape(depth, 1, n_out))


def _even_in_kernel(x_ref, mod_ref, g_ref, w_ref, o_ref, *, n_lat_tiles, tiles_per_batch, nb):
    i = pl.program_id(0)
    row = _mod_row(i, n_lat_tiles, tiles_per_batch, nb)
    sh = mod_ref[pl.ds(row, 1), 0:D_MODEL]
    sc = mod_ref[pl.ds(row, 1), D_MODEL:2 * D_MODEL]
    h = _rms(x_ref[...], g_ref[...]) * (1.0 + sc) + sh
    u = _dot(h.astype(BF16), w_ref[...])
    o_ref[:, 0:D_LRU] = u[:, 0:D_LRU]
    o_ref[:, D_LRU:2 * D_LRU] = jax.nn.gelu(u[:, D_LRU:2 * D_LRU])
    val = u[:, 2 * D_LRU:2 * D_LRU + D_CONV]
    gate = u[:, 2 * D_LRU + D_CONV:]
    o_ref[:, 2 * D_LRU:] = val * jax.nn.sigmoid(gate)


def _even_in(x, mod_l, g, w_in, dims, interpret):
    nb, s_len, c_len = dims
    t = x.shape[0]
    n_out = 2 * D_LRU + D_CONV
    kern = functools.partial(_even_in_kernel, n_lat_tiles=nb * s_len // TM,
                             tiles_per_batch=s_len // TM, nb=nb)
    return pl.pallas_call(
        kern,
        out_shape=jax.ShapeDtypeStruct((t, n_out), F32),
        grid=(t // TM,),
        in_specs=[
            pl.BlockSpec((TM, D_MODEL), lambda i: (i, 0)),
            pl.BlockSpec(mod_l.shape, lambda i: (0, 0)),
            pl.BlockSpec((1, D_MODEL), lambda i: (0, 0)),
            pl.BlockSpec(w_in.shape, lambda i: (0, 0)),
        ],
        out_specs=pl.BlockSpec((TM, n_out), lambda i: (i, 0)),
        compiler_params=_cparams(("arbitrary",)),
        name="even_in",
        interpret=interpret,
    )(x, mod_l, g.reshape(1, D_MODEL), w_in)


def _block_scan(a, b, reverse):
    n, width = a.shape
    a = a.reshape(n // SUBLANES, SUBLANES, width)
    b = b.reshape(n // SUBLANES, SUBLANES, width)
    tmod = lax.broadcasted_iota(jnp.int32, a.shape, 1)
    d = 1
    while d < SUBLANES:
        shift = SUBLANES - d if reverse else d
        keep = (tmod < SUBLANES - d) if reverse else (tmod >= d)
        a_n = pltpu.roll(a, shift, axis=1)
        b_n = pltpu.roll(b, shift, axis=1)
        b = a * jnp.where(keep, b_n, 0.0) + b
        a = a * jnp.where(keep, a_n, 1.0)
        d *= 2
    return a.reshape(n, width), b.reshape(n, width)


def _lru_coeffs(u, ub, wr, br, wi, bi, nsp):
    r = jax.nn.sigmoid(_dot(ub, wr) + br)
    i = jax.nn.sigmoid(_dot(ub, wi) + bi)
    log_a = (-LRU_C) * r * nsp
    a = jnp.exp(log_a)
    b = jnp.sqrt(1.0 - a * a) * (i * u)
    return a, b


def _seq_kernel(xl_ref, xc_ref, gl_ref, gc_ref, vl_ref, vc_ref,
                cw_ref, cb_ref, wr_ref, br_ref, wi_ref, bi_ref, sp_ref, dw_ref, db_ref,
                ol_ref, oc_ref,
                af_ref, bf_ref, ar_ref, br_sc_ref, hf_ref, hr_ref, xp_ref, ph_ref, *, n_lru_tiles):
    c = pl.program_id(1)
    s_len = xl_ref.shape[0]
    c_len = xc_ref.shape[0]

    @pl.when(c < n_lru_tiles)
    def _lru():
        cw = cw_ref[...]
        cb = cb_ref[...]
        nsp = [jax.nn.softplus(-sp_ref[d]) for d in range(2)]

        def coeffs(x_ref, n):
            zpad = jnp.zeros((SUBLANES, CT), F32)
            xp_ref[0:SUBLANES] = zpad
            xp_ref[SUBLANES:SUBLANES + n] = x_ref[...]
            xp_ref[SUBLANES + n:2 * SUBLANES + n] = zpad
            ch = min(LRU_CHUNK, n)

            def body(j, _):
                t0 = pl.multiple_of(j * ch, ch)
                win = xp_ref[pl.ds(t0, ch + 2 * SUBLANES)]
                off = SUBLANES - LRU_CONV_W // 2
                u = cb
                for k in range(LRU_CONV_W):
                    u = u + cw[k:k + 1] * win[off + k:off + k + ch]
                ub = u.astype(BF16)
                a_f, b_f = _lru_coeffs(u, ub, wr_ref[0], br_ref[0], wi_ref[0], bi_ref[0], nsp[0])
                a_f, b_f = _block_scan(a_f, b_f, False)
                af_ref[pl.ds(t0, ch)] = a_f
                bf_ref[pl.ds(t0, ch)] = b_f
                a_r, b_r = _lru_coeffs(u, ub, wr_ref[1], br_ref[1], wi_ref[1], bi_ref[1], nsp[1])
                a_r, b_r = _block_scan(a_r, b_r, True)
                ar_ref[pl.ds(t0, ch)] = a_r
                br_sc_ref[pl.ds(t0, ch)] = b_r
                return 0

            lax.fori_loop(0, n // ch, body, 0)

        def carry_pass(n, h0f, h0r):
            nblk = n // SUBLANES

            def body(k, carry):
                hf, hr = carry
                kf = pl.multiple_of(k * SUBLANES, SUBLANES)
                kr = pl.multiple_of((nblk - 1 - k) * SUBLANES, SUBLANES)
                yf = bf_ref[pl.ds(kf, SUBLANES)] + af_ref[pl.ds(kf, SUBLANES)] * hf
                yr = br_sc_ref[pl.ds(kr, SUBLANES)] + ar_ref[pl.ds(kr, SUBLANES)] * hr
                hf_ref[pl.ds(kf, SUBLANES)] = yf
                hr_ref[pl.ds(kr, SUBLANES)] = yr
                return yf[SUBLANES - 1:SUBLANES], yr[0:1]

            return lax.fori_loop(0, nblk, body, (h0f, h0r))

        zero = jnp.zeros((1, CT), F32)
        coeffs(xc_ref, c_len)
        hcf, hcr = carry_pass(c_len, zero, zero)
        oc_ref[...] = (hf_ref[0:c_len] + hr_ref[0:c_len]) * gc_ref[...]
        coeffs(xl_ref, s_len)
        carry_pass(s_len, hcf, hcr)
        ol_ref[...] = (hf_ref[0:s_len] + hr_ref[0:s_len]) * gl_ref[...]

    @pl.when(c >= n_lru_tiles)
    def _conv():
        dw = dw_ref[...]
        db = db_ref[...]
        zpad = jnp.zeros((CONF_PAD, CT), F32)

        def conv(v_ref, o_ref, n):
            xp_ref[0:CONF_PAD] = zpad
            xp_ref[CONF_PAD:CONF_PAD + n] = v_ref[...]
            xp_ref[CONF_PAD + n:2 * CONF_PAD + n] = zpad

            def body(j, _):
                t0 = pl.multiple_of(j * CONV_CHUNK, CONV_CHUNK)
                win = xp_ref[pl.ds(t0, CONV_CHUNK + 2 * CONF_PAD)]
                acc = jnp.broadcast_to(db, (CONV_CHUNK, CT))
                off = CONF_PAD - CONF_WIDTH // 2
                span = CONV_CHUNK + 2 * CONF_PAD - SUBLANES
                for r in range(SUBLANES):
                    ph_ref[r] = win[r:r + span]
                for k in range(CONF_WIDTH):
                    r = (off + k) % SUBLANES
                    q = (off + k) // SUBLANES * SUBLANES
                    acc = acc + dw[k:k + 1] * ph_ref[r, q:q + CONV_CHUNK]
                o_ref[pl.ds(t0, CONV_CHUNK)] = acc
                return 0

            lax.fori_loop(0, n // CONV_CHUNK, body, 0)

        conv(vc_ref, oc_ref, c_len)
        conv(vl_ref, ol_ref, s_len)


def _seq_mix(u_all, p, dims, interpret):
    nb, s_len, c_len = dims
    n_lru = D_LRU // CT
    n_cv = D_CONV // CT
    lat_blocks = nb * s_len // c_len

    def lru_c(c):
        return jnp.minimum(c, n_lru - 1)

    def cv_c(c):
        return jnp.maximum(c - n_lru, 0)

    in_specs = [
        pl.BlockSpec((s_len, CT), lambda b, c: (b, lru_c(c))),
        pl.BlockSpec((c_len, CT), lambda b, c: (lat_blocks + b, lru_c(c))),
        pl.BlockSpec((s_len, CT), lambda b, c: (b, n_lru + lru_c(c))),
        pl.BlockSpec((c_len, CT), lambda b, c: (lat_blocks + b, n_lru + lru_c(c))),
        pl.BlockSpec((s_len, CT), lambda b, c: (b, 2 * n_lru + cv_c(c))),
        pl.BlockSpec((c_len, CT), lambda b, c: (lat_blocks + b, 2 * n_lru + cv_c(c))),
        pl.BlockSpec((LRU_CONV_W, CT), lambda b, c: (0, lru_c(c))),
        pl.BlockSpec((1, CT), lambda b, c: (0, lru_c(c))),
        pl.BlockSpec((2, None, CT, CT), lambda b, c: (0, lru_c(c), 0, 0)),
        pl.BlockSpec((2, 1, CT), lambda b, c: (0, 0, lru_c(c))),
        pl.BlockSpec((2, None, CT, CT), lambda b, c: (0, lru_c(c), 0, 0)),
        pl.BlockSpec((2, 1, CT), lambda b, c: (0, 0, lru_c(c))),
        pl.BlockSpec((2, 1, CT), lambda b, c: (0, 0, lru_c(c))),
        pl.BlockSpec((CONF_WIDTH, CT), lambda b, c: (0, cv_c(c))),
        pl.BlockSpec((1, CT), lambda b, c: (0, cv_c(c))),
    ]
    out_specs = [
        pl.BlockSpec((s_len, CT), lambda b, c: (b, c)),
        pl.BlockSpec((c_len, CT), lambda b, c: (b, c)),
    ]
    scratch = [pltpu.VMEM((s_len, CT), F32) for _ in range(6)]
    scratch.append(pltpu.VMEM((s_len + 2 * CONF_PAD, CT), F32))
    scratch.append(pltpu.VMEM((SUBLANES, CONV_CHUNK + 2 * CONF_PAD - SUBLANES, CT), F32))
    return pl.pallas_call(
        functools.partial(_seq_kernel, n_lru_tiles=n_lru),
        out_shape=[jax.ShapeDtypeStruct((nb * s_len, D_LRU + D_CONV), F32),
                   jax.ShapeDtypeStruct((nb * c_len, D_LRU + D_CONV), F32)],
        grid=(nb, n_lru + n_cv),
        in_specs=in_specs,
        out_specs=out_specs,
        scratch_shapes=scratch,
        compiler_params=_cparams(("arbitrary", "arbitrary")),
        name="seq_mix",
        interpret=interpret,
    )(u_all, u_all, u_all, u_all, u_all, u_all,
      p["conv_w"], p["conv_b"], p["w_r"], p["b_r"], p["w_i"], p["b_i"], p["lam"], p["dw_w"], p["dw_b"])


def _swap32(x):
    lane = lax.broadcasted_iota(jnp.int32, x.shape, 1)
    return jnp.where((lane & 32) == 0, pltpu.roll(x, HEAD_DIM - 32, axis=1), pltpu.roll(x, 32, axis=1))


def _odd_in_kernel(x_ref, mod_ref, g_ref, w_ref, qg_ref, kg_ref, cos_ref, sin_ref, o_ref,
                   *, n_lat_tiles, tiles_per_batch, nb):
    i = pl.program_id(0)
    row = _mod_row(i, n_lat_tiles, tiles_per_batch, nb)
    sh = mod_ref[pl.ds(row, 1), 0:D_MODEL]
    sc = mod_ref[pl.ds(row, 1), D_MODEL:2 * D_MODEL]
    h = _rms(x_ref[...], g_ref[...]) * (1.0 + sc) + sh
    u = _dot(h.astype(BF16), w_ref[...])
    is_lat = i < n_lat_tiles
    cos = jnp.where(is_lat, cos_ref[...], 1.0)
    sin = jnp.where(is_lat, sin_ref[...], 0.0)
    q_scale = math.log2(math.e) / math.sqrt(HEAD_DIM)
    for hh in range(N_Q_HEADS + N_KV_HEADS):
        xh = u[:, hh * HEAD_DIM:(hh + 1) * HEAD_DIM]
        gain = qg_ref[...] if hh < N_Q_HEADS else kg_ref[...]
        xn = _rms(xh, gain)
        y = xn * cos + _swap32(xn) * sin
        if hh < N_Q_HEADS:
            y = y * q_scale
        o_ref[:, hh * HEAD_DIM:(hh + 1) * HEAD_DIM] = y.astype(BF16)
    v0 = (N_Q_HEADS + N_KV_HEADS) * HEAD_DIM
    o_ref[:, v0:] = u[:, v0:].astype(BF16)


def _odd_in(x, mod_l, g, w_qkv, q_g, k_g, rope_cos, rope_sin, dims, interpret):
    nb, s_len, c_len = dims
    t = x.shape[0]
    n_out = w_qkv.shape[1]
    tpb = s_len // TM
    n_lat = nb * tpb
    kern = functools.partial(_odd_in_kernel, n_lat_tiles=n_lat, tiles_per_batch=tpb, nb=nb)

    def pos_block(i):
        return jnp.where(i < n_lat, i % tpb, 0)

    return pl.pallas_call(
        kern,
        out_shape=jax.ShapeDtypeStruct((t, n_out), BF16),
        grid=(t // TM,),
        in_specs=[
            pl.BlockSpec((TM, D_MODEL), lambda i: (i, 0)),
            pl.BlockSpec(mod_l.shape, lambda i: (0, 0)),
            pl.BlockSpec((1, D_MODEL), lambda i: (0, 0)),
            pl.BlockSpec(w_qkv.shape, lambda i: (0, 0)),
            pl.BlockSpec((1, HEAD_DIM), lambda i: (0, 0)),
            pl.BlockSpec((1, HEAD_DIM), lambda i: (0, 0)),
            pl.BlockSpec((TM, HEAD_DIM), lambda i: (pos_block(i), 0)),
            pl.BlockSpec((TM, HEAD_DIM), lambda i: (pos_block(i), 0)),
        ],
        out_specs=pl.BlockSpec((TM, n_out), lambda i: (i, 0)),
        compiler_params=_cparams(("arbitrary",)),
        name="odd_in",
        interpret=interpret,
    )(x, mod_l, g.reshape(1, D_MODEL), w_qkv, q_g.reshape(1, HEAD_DIM), k_g.reshape(1, HEAD_DIM),
      rope_cos, rope_sin)


def _attn_kernel(*refs, n_seg):
    q_ref = refs[0]
    k_refs = refs[1:1 + n_seg]
    v_refs = refs[1 + n_seg:1 + 2 * n_seg]
    o_ref = refs[1 + 2 * n_seg]
    tq = q_ref.shape[0]
    chunks = []
    for k_ref, v_ref in zip(k_refs, v_refs):
        n = k_ref.shape[0]
        step = min(TK, n)
        for s0 in range(0, n, step):
            chunks.append((k_ref, v_ref, s0, step))
    for g in range(GQA_GROUP):
        q = q_ref[:, g * HEAD_DIM:(g + 1) * HEAD_DIM]
        m = jnp.full((tq, 1), -jnp.inf, F32)
        acc = jnp.zeros((tq, 2 * HEAD_DIM), F32)
        for k_ref, v_ref, s0, step in chunks:
            k = k_ref[s0:s0 + step, :]
            v1 = jnp.concatenate([v_ref[s0:s0 + step, :], jnp.ones((step, HEAD_DIM), BF16)], axis=1)
            s = lax.dot_general(q, k, (((1,), (1,)), ((), ())), preferred_element_type=F32)
            m_new = jnp.maximum(m, jnp.max(s, axis=-1, keepdims=True))
            p = jnp.exp2(s - m_new)
            acc = jnp.exp2(m - m_new) * acc + _dot(p.astype(BF16), v1)
            m = m_new
        o_ref[:, g * HEAD_DIM:(g + 1) * HEAD_DIM] = (acc[:, 0:HEAD_DIM] / acc[:, HEAD_DIM:HEAD_DIM + 1]).astype(BF16)


def _attention(qkv, dims, latent, interpret):
    nb, s_len, c_len = dims
    k_col = N_Q_HEADS
    v_col = N_Q_HEADS + N_KV_HEADS
    lat_blocks = nb * s_len // c_len
    ctx_k = pl.BlockSpec((c_len, HEAD_DIM), lambda b, h, qi: (lat_blocks + b, k_col + h))
    ctx_v = pl.BlockSpec((c_len, HEAD_DIM), lambda b, h, qi: (lat_blocks + b, v_col + h))
    if latent:
        tq = TQ
        nq = s_len // tq
        q_spec = pl.BlockSpec((tq, GQA_GROUP * HEAD_DIM), lambda b, h, qi: (b * nq + qi, h))
        k_specs = [pl.BlockSpec((s_len, HEAD_DIM), lambda b, h, qi: (b, k_col + h)), ctx_k]
        v_specs = [pl.BlockSpec((s_len, HEAD_DIM), lambda b, h, qi: (b, v_col + h)), ctx_v]
        out_rows = nb * s_len
        o_spec = pl.BlockSpec((tq, GQA_GROUP * HEAD_DIM), lambda b, h, qi: (b * nq + qi, h))
    else:
        tq = c_len
        nq = 1
        q_spec = pl.BlockSpec((tq, GQA_GROUP * HEAD_DIM), lambda b, h, qi: (lat_blocks + b, h))
        k_specs = [ctx_k]
        v_specs = [ctx_v]
        out_rows = nb * c_len
        o_spec = pl.BlockSpec((tq, GQA_GROUP * HEAD_DIM), lambda b, h, qi: (b, h))
    n_seg = len(k_specs)
    return pl.pallas_call(
        functools.partial(_attn_kernel, n_seg=n_seg),
        out_shape=jax.ShapeDtypeStruct((out_rows, N_Q_HEADS * HEAD_DIM), BF16),
        grid=(nb, N_KV_HEADS, nq),
        in_specs=[q_spec] + k_specs + v_specs,
        out_specs=o_spec,
        compiler_params=_cparams(("arbitrary", "arbitrary", "arbitrary")),
        name="attn_lat" if latent else "attn_ctx",
        interpret=interpret,
    )(*([qkv] * (1 + 2 * n_seg)))


def _route(logits_t, bias):
    score = [jax.nn.sigmoid(logits_t[e:e + 1]) for e in range(N_EXPERTS)]
    sel = [score[e] + bias[e] for e in range(N_EXPERTS)]
    grp = []
    for g in range(N_GROUPS):
        a, b, c, d = sel[4 * g:4 * g + 4]
        hi1, lo1 = jnp.maximum(a, b), jnp.minimum(a, b)
        hi2, lo2 = jnp.maximum(c, d), jnp.minimum(c, d)
        top1 = jnp.maximum(hi1, hi2)
        top2 = jnp.maximum(jnp.minimum(hi1, hi2), jnp.maximum(lo1, lo2))
        grp.append(top1 + top2)
    best = jnp.zeros_like(grp[0], dtype=jnp.int32)
    best_s = grp[0]
    for g in range(1, N_GROUPS):
        better = grp[g] > best_s
        best = jnp.where(better, g, best)
        best_s = jnp.where(better, grp[g], best_s)
    chosen = []
    for e in range(N_EXPERTS):
        g = e // EXPERTS_PER_GROUP
        beaten = jnp.zeros_like(best)
        for k in range(4 * g, 4 * g + 4):
            if k == e:
                continue
            beats = (sel[k] >= sel[e]) if k < e else (sel[k] > sel[e])
            beaten = beaten + jnp.where(beats, 1, 0)
        chosen.append((beaten < 2) & (best == g))
    wsum = jnp.zeros_like(score[0])
    for e in range(N_EXPERTS):
        wsum = wsum + jnp.where(chosen[e], score[e], 0.0)
    picked, gate = [], []
    for j in range(EXPERTS_PER_GROUP):
        cj = jnp.zeros_like(best)
        gj = jnp.zeros_like(wsum)
        for g in range(N_GROUPS):
            e = EXPERTS_PER_GROUP * g + j
            cj = jnp.where(chosen[e], 1, cj)
            gj = jnp.where(best == g, score[e], gj)
        picked.append(cj > 0)
        gate.append(gj / wsum)
    j_lo = jnp.where(picked[0], 0, jnp.where(picked[1], 1, 2))
    j_hi = jnp.where(picked[3], 3, jnp.where(picked[2], 2, 1))
    pair = jnp.where(j_lo == 0, j_hi - 1, jnp.where(j_lo == 1, j_hi + 1, PAIRS_PER_GROUP - 1))
    gate_lo = jnp.where(j_lo == 0, gate[0], jnp.where(j_lo == 1, gate[1], gate[2]))
    gate_hi = jnp.where(j_hi == 3, gate[3], jnp.where(j_hi == 2, gate[2], gate[1]))
    return best * PAIRS_PER_GROUP + pair, gate_lo, gate_hi


def _mix_out_kernel(yl_ref, yc_ref, x_ref, mod_ref, w_ref, lng_ref, lnb_ref, nfg_ref, rwh_ref, rwl_ref, rb_ref,
                    xo_ref, hrow_ref, pos_ref, cnt_ref, base_ref,
                    *, even, n_lat_tiles, tiles_per_batch, nb, cap):
    i = pl.program_id(0)

    @pl.when(i == 0)
    def _init():
        base_ref[...] = jnp.zeros_like(base_ref)

    row = _mod_row(i, n_lat_tiles, tiles_per_batch, nb)
    g1 = mod_ref[pl.ds(row, 1), 2 * D_MODEL:3 * D_MODEL]
    sh2 = mod_ref[pl.ds(row, 1), 3 * D_MODEL:4 * D_MODEL]
    sc2 = mod_ref[pl.ds(row, 1), 4 * D_MODEL:5 * D_MODEL]
    is_lat = i < n_lat_tiles
    y = jnp.where(is_lat, yl_ref[...], yc_ref[...])
    if even:
        ya = y[:, 0:D_LRU]
        vc = y[:, D_LRU:]
        mu = jnp.mean(vc, axis=-1, keepdims=True)
        xc = vc - mu
        var = jnp.mean(xc * xc, axis=-1, keepdims=True)
        ln = xc * lax.rsqrt(var + NORM_EPS) * lng_ref[...] + lnb_ref[...]
        yb = ln * jax.nn.sigmoid(ln)
        out = _dot(ya.astype(BF16), w_ref[0:D_LRU, :]) + _dot(yb.astype(BF16), w_ref[D_LRU:, :])
    else:
        out = _dot(y, w_ref[...])
    x_new = x_ref[...] + g1 * out
    xo_ref[...] = x_new
    h2 = _rms(x_new, nfg_ref[...]) * (1.0 + sc2) + sh2
    hrow_ref[:, 0:D_MODEL] = h2
    h_hi, h_lo = _split_bf16(h2)
    logits = _dot3(h_hi, h_lo, rwh_ref[...], rwl_ref[...])
    logits_t = jnp.transpose(logits)[0:N_EXPERTS]
    cls, gate_lo, gate_hi = _route(logits_t, [rb_ref[e] for e in range(N_EXPERTS)])
    tm = logits.shape[0]

    sub = lax.broadcasted_iota(jnp.int32, (CLASS_ROWS, tm), 0)
    onehot = jnp.where(sub == cls, 1.0, 0.0)
    earlier = (lax.broadcasted_iota(jnp.int32, (tm, tm), 0) < lax.broadcasted_iota(jnp.int32, (tm, tm), 1))
    before = _dot(onehot.astype(BF16), jnp.where(earlier, 1.0, 0.0).astype(BF16))
    rank = jnp.sum(onehot * (before + base_ref[:, 0:1]), axis=0, keepdims=True)
    pos_ref[...] = cls * cap + rank.astype(jnp.int32)
    base_ref[...] = base_ref[...] + jnp.sum(onehot, axis=1, keepdims=True)
    cnt_ref[...] = base_ref[...]

    sub8 = lax.broadcasted_iota(jnp.int32, (SUBLANES, tm), 0)
    extras_t = jnp.where(sub8 == 0, gate_lo, jnp.where(sub8 == 1, gate_hi, 0.0))
    extras_t = jnp.concatenate([extras_t, jnp.zeros((LANES - SUBLANES, tm), F32)], axis=0)
    hrow_ref[:, D_MODEL:] = jnp.transpose(extras_t)


def _mix_out(y_lat, y_ctx, x, mod_l, w_out, ln_g, ln_b, nf_g, rw_hi, rw_lo, router_bias, dims, n_tiles, cap, even,
             interpret):
    nb, s_len, c_len = dims
    tpb = s_len // TM
    n_lat = nb * tpb
    n_ctx = max(nb * c_len // TM, 1)
    rows = n_tiles * TM
    kern = functools.partial(_mix_out_kernel, even=even, n_lat_tiles=n_lat, tiles_per_batch=tpb, nb=nb, cap=cap)
    vec = lambda n: pl.BlockSpec((1, n), lambda i: (0, 0))
    return pl.pallas_call(
        kern,
        out_shape=[jax.ShapeDtypeStruct((rows, D_MODEL), F32),
                   jax.ShapeDtypeStruct((rows, ROW_W), F32),
                   jax.ShapeDtypeStruct((n_tiles, 1, TM), jnp.int32),
                   jax.ShapeDtypeStruct((CLASS_ROWS, LANES), F32)],
        grid=(n_tiles,),
        in_specs=[
            pl.BlockSpec((TM, D_MODEL), lambda i: (jnp.minimum(i, n_lat - 1), 0)),
            pl.BlockSpec((TM, D_MODEL), lambda i: (jnp.clip(i - n_lat, 0, n_ctx - 1), 0)),
            pl.BlockSpec((TM, D_MODEL), lambda i: (i, 0)),
            pl.BlockSpec(mod_l.shape, lambda i: (0, 0)),
            pl.BlockSpec(w_out.shape, lambda i: (0, 0)),
            vec(D_CONV), vec(D_CONV), vec(D_MODEL),
            pl.BlockSpec(rw_hi.shape, lambda i: (0, 0)),
            pl.BlockSpec(rw_lo.shape, lambda i: (0, 0)),
            pl.BlockSpec(memory_space=pltpu.SMEM),
        ],
        out_specs=[
            pl.BlockSpec((TM, D_MODEL), lambda i: (i, 0)),
            pl.BlockSpec((TM, ROW_W), lambda i: (i, 0)),
            pl.BlockSpec((None, 1, TM), lambda i: (i, 0, 0)),
            pl.BlockSpec((CLASS_ROWS, LANES), lambda i: (0, 0)),
        ],
        scratch_shapes=[pltpu.VMEM((CLASS_ROWS, LANES), F32)],
        compiler_params=_cparams(("arbitrary",)),
        name="mix_out_even" if even else "mix_out_odd",
        interpret=interpret,
    )(y_lat, y_ctx, x, mod_l, w_out, ln_g.reshape(1, D_CONV), ln_b.reshape(1, D_CONV),
      nf_g.reshape(1, D_MODEL), rw_hi, rw_lo, router_bias)


def _row_copy(src_ref, src_row, dst_ref, dst_row, sem):
    return pltpu.make_async_copy(src_ref.at[pl.ds(src_row, 1)], dst_ref.at[pl.ds(dst_row, 1)], sem)


def _dispatch_kernel(pos_ref, fill_ref, nact_ref, h_ref, xs_ref, zero_ref, row_sem, fill_sem, tail_sem, *, n_slots):
    i = pl.program_id(0)
    tm = h_ref.shape[0]

    @pl.when(i == 0)
    def _fill():
        zero_ref[...] = jnp.zeros_like(zero_ref)
        def fill(tile, sem):
            return pltpu.make_async_copy(zero_ref, xs_ref.at[pl.ds(pl.multiple_of(tile * TMS, TMS), TMS)], sem)

        for phase in ("start", "wait"):
            for c in range(N_CLASSES):
                @pl.when(fill_ref[c] >= 0)
                def _last_tile():
                    getattr(fill(fill_ref[c], fill_sem.at[c]), phase)()

                @pl.when(n_slots - 1 - c >= nact_ref[0])
                def _unused_tile():
                    getattr(fill(n_slots - 1 - c, tail_sem.at[c]), phase)()

    def issue(j, _):
        for k in range(ISSUE_UNROLL):
            r = j * ISSUE_UNROLL + k
            _row_copy(h_ref, r, xs_ref, pos_ref[i * tm + r], row_sem).start()
        return 0

    lax.fori_loop(0, tm // ISSUE_UNROLL, issue, 0)
    pltpu.make_async_copy(h_ref, xs_ref.at[pl.ds(0, tm)], row_sem).wait()


def _dispatch(hrow, pos_flat, fill_tile, n_active, n_slots, td, interpret):
    rows = hrow.shape[0]
    return pl.pallas_call(
        functools.partial(_dispatch_kernel, n_slots=n_slots),
        out_shape=jax.ShapeDtypeStruct((n_slots * TMS, ROW_W), F32),
        grid_spec=pltpu.PrefetchScalarGridSpec(
            num_scalar_prefetch=3,
            grid=(rows // td,),
            in_specs=[pl.BlockSpec((td, ROW_W), lambda i, pos, fill, nact: (i, 0))],
            out_specs=pl.BlockSpec(memory_space=pl.ANY),
            scratch_shapes=[pltpu.VMEM((TMS, ROW_W), F32), pltpu.SemaphoreType.DMA,
                            pltpu.SemaphoreType.DMA((N_CLASSES,)), pltpu.SemaphoreType.DMA((N_CLASSES,))],
        ),
        compiler_params=_cparams(("arbitrary",)),
        name="moe_dispatch",
        interpret=interpret,
    )(pos_flat, fill_tile, n_active, hrow)


def _experts_kernel(elo_ref, ehi_ref, nact_ref, xs_ref, wg0_ref, wu0_ref, wd0_ref, wg1_ref, wu1_ref, wd1_ref,
                    ys_ref):
    i = pl.program_id(0)

    @pl.when(i < nact_ref[0])
    def _active():
        h = xs_ref[:, 0:D_MODEL].astype(BF16)
        out = None
        for lane, (wg_ref, wu_ref, wd_ref) in enumerate(((wg0_ref, wu0_ref, wd0_ref),
                                                         (wg1_ref, wu1_ref, wd1_ref))):
            a = _dot(h, wg_ref[...])
            u = _dot(h, wu_ref[...])
            hid = (a * jax.nn.sigmoid(a)) * u * xs_ref[:, D_MODEL + lane:D_MODEL + lane + 1]
            y = _dot(hid.astype(BF16), wd_ref[...])
            out = y if out is None else out + y
        ys_ref[...] = out

    @pl.when(i >= nact_ref[0])
    def _unused():
        ys_ref[...] = jnp.zeros_like(ys_ref)


def _experts(xs, tile_lo, tile_hi, n_active, wg, wu, wd, interpret):
    n_slots = tile_lo.shape[0]
    lo_idx = lambda i, lo, hi, nact: (lo[i], 0, 0)
    hi_idx = lambda i, lo, hi, nact: (hi[i], 0, 0)
    w_in = (None, D_MODEL, D_EXPERT)
    w_out = (None, D_EXPERT, D_MODEL)
    return pl.pallas_call(
        _experts_kernel,
        out_shape=jax.ShapeDtypeStruct((n_slots * TMS, D_MODEL), F32),
        grid_spec=pltpu.PrefetchScalarGridSpec(
            num_scalar_prefetch=3,
            grid=(n_slots,),
            in_specs=[
                pl.BlockSpec((TMS, ROW_W), lambda i, lo, hi, nact: (jnp.minimum(i, nact[0] - 1), 0)),
                pl.BlockSpec(w_in, lo_idx), pl.BlockSpec(w_in, lo_idx), pl.BlockSpec(w_out, lo_idx),
                pl.BlockSpec(w_in, hi_idx), pl.BlockSpec(w_in, hi_idx), pl.BlockSpec(w_out, hi_idx),
            ],
            out_specs=pl.BlockSpec((TMS, D_MODEL), lambda i, lo, hi, nact: (i, 0)),
        ),
        compiler_params=_cparams(("arbitrary",)),
        name="moe_experts",
        interpret=interpret,
    )(tile_lo, tile_hi, n_active, xs, wg, wu, wd, wg, wu, wd)


def _combine_kernel(pos_ref, ys_ref, x_ref, mod_ref, fg_ref, o_ref, ybuf_ref, sem,
                    *, final, n_lat_tiles, tiles_per_batch, nb):
    i = pl.program_id(0)
    tm = x_ref.shape[0]

    def issue(j, _):
        for k in range(ISSUE_UNROLL):
            r = j * ISSUE_UNROLL + k
            _row_copy(ys_ref, pos_ref[i * tm + r], ybuf_ref, r, sem).start()
        return 0

    lax.fori_loop(0, tm // ISSUE_UNROLL, issue, 0)
    pltpu.make_async_copy(ys_ref.at[pl.ds(0, tm)], ybuf_ref, sem).wait()
    row = _mod_row(i, n_lat_tiles, tiles_per_batch, nb)
    g2 = mod_ref[pl.ds(row, 1), 5 * D_MODEL:6 * D_MODEL]
    x_new = x_ref[...] + g2 * ybuf_ref[...]
    if final:
        x_new = _rms(x_new, fg_ref[...])
    o_ref[...] = x_new


def _combine(ys, pos_flat, x, mod_l, final_g, dims, td, final, interpret):
    nb, s_len, c_len = dims
    rows = x.shape[0]
    tpb = s_len // td
    kern = functools.partial(_combine_kernel, final=final, n_lat_tiles=nb * tpb, tiles_per_batch=tpb, nb=nb)
    return pl.pallas_call(
        kern,
        out_shape=jax.ShapeDtypeStruct((rows, D_MODEL), F32),
        grid_spec=pltpu.PrefetchScalarGridSpec(
            num_scalar_prefetch=1,
            grid=(rows // td,),
            in_specs=[
                pl.BlockSpec(memory_space=pl.ANY),
                pl.BlockSpec((td, D_MODEL), lambda i, pos: (i, 0)),
                pl.BlockSpec(mod_l.shape, lambda i, pos: (0, 0)),
                pl.BlockSpec((1, D_MODEL), lambda i, pos: (0, 0)),
            ],
            out_specs=pl.BlockSpec((td, D_MODEL), lambda i, pos: (i, 0)),
            scratch_shapes=[pltpu.VMEM((td, D_MODEL), F32), pltpu.SemaphoreType.DMA],
        ),
        compiler_params=_cparams(("arbitrary",)),
        name="moe_combine_final" if final else "moe_combine",
        interpret=interpret,
    )(pos_flat, ys, x, mod_l, final_g.reshape(1, D_MODEL))


def _sorted_layout(pos, counts, cap, n_slots):
    cnt = counts[0:N_CLASSES, 0].astype(jnp.int32)
    tiles = (cnt + TMS - 1) // TMS
    ends = jnp.cumsum(tiles)
    starts = (ends - tiles) * TMS
    n_active = ends[-1]
    pos_flat = pos.reshape(-1)
    pos_sorted = jnp.take(starts, pos_flat // cap) + pos_flat % cap
    fill_tile = jnp.where(tiles > 0, ends - 1, -1)
    slot = jnp.minimum(jnp.arange(n_slots, dtype=jnp.int32), n_active - 1)
    tile_cls = jnp.sum((slot[:, None] >= ends[None, :]).astype(jnp.int32), axis=1)
    pair_lo = jnp.array([0, 0, 0, 1, 1, 2], jnp.int32)
    pair_hi = jnp.array([1, 2, 3, 2, 3, 3], jnp.int32)
    first = (tile_cls // PAIRS_PER_GROUP) * EXPERTS_PER_GROUP
    tile_lo = first + jnp.take(pair_lo, tile_cls % PAIRS_PER_GROUP)
    tile_hi = first + jnp.take(pair_hi, tile_cls % PAIRS_PER_GROUP)
    i32 = lambda a: a.astype(jnp.int32)
    return i32(pos_sorted), i32(fill_tile), i32(tile_lo), i32(tile_hi), i32(n_active.reshape(1))


def _moe(x_mid, hrow, pos, counts, mod_l, wg, wu, wd, final_g, dims, cap, final, interpret):
    rows = x_mid.shape[0]
    n_slots = rows // TMS + N_CLASSES
    pos_sorted, fill_tile, tile_lo, tile_hi, n_active = _sorted_layout(pos, counts, cap, n_slots)
    nb, s_len, c_len = dims
    td = TD if (s_len % TD == 0 and (nb * c_len) % TD == 0) else TM
    xs = _dispatch(hrow, pos_sorted, fill_tile, n_active, n_slots, td, interpret)
    ys = _experts(xs, tile_lo, tile_hi, n_active, wg, wu, wd, interpret)
    return _combine(ys, pos_sorted, x_mid, mod_l, final_g, dims, td, final, interpret)


def _rope_tables(s_len):
    rows = s_len // GRID_W
    quarter = HEAD_DIM // 4
    r, col = jnp.meshgrid(jnp.arange(rows, dtype=F32), jnp.arange(GRID_W, dtype=F32), indexing="ij")
    inv_freq = ROPE_THETA ** (-jnp.arange(0, HEAD_DIM // 2, 2, dtype=F32) / (HEAD_DIM // 2))
    ang_r = r.reshape(-1)[:, None] * inv_freq
    ang_c = col.reshape(-1)[:, None] * inv_freq
    cos = jnp.concatenate([jnp.cos(ang_r)] * 2 + [jnp.cos(ang_c)] * 2, axis=-1)
    sin = jnp.concatenate([-jnp.sin(ang_r), jnp.sin(ang_r), -jnp.sin(ang_c), jnp.sin(ang_c)], axis=-1)
    assert cos.shape == (s_len, 4 * quarter)
    return cos, sin


def _block_diag_tiles(w):
    per = CT // LRU_HEAD_DIM
    n_tiles = D_LRU // CT
    w = w.reshape(2, n_tiles, per, LRU_HEAD_DIM, LRU_HEAD_DIM)
    eye = jnp.eye(per, dtype=w.dtype)
    full = jnp.einsum("dtpij,pq->dtpiqj", w, eye)
    return full.reshape(2, n_tiles, CT, CT).astype(BF16)


def _forward(x, c, ctx, c_ctx, ada_w, ada_b, norm_mix_g, norm_ffn_g, ev_w_in, ev_w_out,
             lru_conv_w, lru_conv_b, lru_w_r, lru_b_r, lru_w_i, lru_b_i, lru_lambda,
             cf_dw_w, cf_dw_b, cf_ln_g, cf_ln_b, at_w_qkv, at_w_out, at_q_norm_g, at_k_norm_g,
             router_w, router_bias, exp_w_gate, exp_w_up, exp_w_down, final_norm_g, interpret=False):
    nb, s_len, d = x.shape
    c_len = ctx.shape[1]
    depth = ada_w.shape[0]
    dims = (nb, s_len, c_len)
    assert d == D_MODEL and nb < SUBLANES
    assert s_len % TM == 0 and (nb * c_len) % TM == 0 and s_len % c_len == 0 and s_len % TQ == 0
    n_lat_tiles = nb * s_len // TM
    n_all_tiles = n_lat_tiles + nb * c_len // TM

    cvec = jnp.zeros((SUBLANES, d), F32).at[:nb].set(c).at[nb].set(c_ctx)
    mod = _adaln(cvec, ada_w, ada_b, interpret)
    xf = jnp.concatenate([x.reshape(nb * s_len, d), ctx.reshape(nb * c_len, d)], axis=0)

    rope_cos, rope_sin = _rope_tables(s_len)
    rw = jnp.zeros((d, LANES), F32).at[:, :N_EXPERTS].set(router_w)
    rw_hi, rw_lo = _split_bf16(rw)
    rb = router_bias.astype(F32)

    for l in range(depth):
        last = l == depth - 1
        n_tiles = n_lat_tiles if last else n_all_tiles
        mod_l = mod[l]
        if l % 2 == 0:
            e = l // 2
            u_all = _even_in(xf, mod_l, norm_mix_g[l], ev_w_in[e].astype(BF16), dims, interpret)
            p = {
                "conv_w": lru_conv_w[e], "conv_b": lru_conv_b[e].reshape(1, D_LRU),
                "w_r": _block_diag_tiles(lru_w_r[e]), "b_r": lru_b_r[e].reshape(2, 1, D_LRU),
                "w_i": _block_diag_tiles(lru_w_i[e]), "b_i": lru_b_i[e].reshape(2, 1, D_LRU),
                "lam": lru_lambda[e].reshape(2, 1, D_LRU),
                "dw_w": cf_dw_w[e], "dw_b": cf_dw_b[e].reshape(1, D_CONV),
            }
            y_lat, y_ctx = _seq_mix(u_all, p, dims, interpret)
            w_out = ev_w_out[e].astype(BF16)
            ln_g, ln_b = cf_ln_g[e], cf_ln_b[e]
        else:
            o = l // 2
            qkv = _odd_in(xf, mod_l, norm_mix_g[l], at_w_qkv[o].astype(BF16), at_q_norm_g[o], at_k_norm_g[o],
                          rope_cos, rope_sin, dims, interpret)
            y_lat = _attention(qkv, dims, True, interpret)
            y_ctx = y_lat if last else _attention(qkv, dims, False, interpret)
            w_out = at_w_out[o].astype(BF16)
            ln_g, ln_b = cf_ln_g[0], cf_ln_b[0]
        cap = n_tiles * TM + TMS
        x_mid, hrow, pos, counts = _mix_out(y_lat, y_ctx, xf, mod_l, w_out, ln_g, ln_b, norm_ffn_g[l], rw_hi, rw_lo,
                                            rb, dims, n_tiles, cap, l % 2 == 0, interpret)
        xf = _moe(x_mid, hrow, pos, counts, mod_l, exp_w_gate[l].astype(BF16), exp_w_up[l].astype(BF16),
                  exp_w_down[l].astype(BF16), final_norm_g, dims, cap, last, interpret)
    return xf.reshape(nb, s_len, d)


def kernel(x, c, ctx, c_ctx, ada_w, ada_b, norm_mix_g, norm_ffn_g, ev_w_in, ev_w_out, lru_conv_w, lru_conv_b,
           lru_w_r, lru_b_r, lru_w_i, lru_b_i, lru_lambda, cf_dw_w, cf_dw_b, cf_ln_g, cf_ln_b, at_w_qkv, at_w_out,
           at_q_norm_g, at_k_norm_g, router_w, router_bias, exp_w_gate, exp_w_up, exp_w_down, final_norm_g):
    return _forward(x, c, ctx, c_ctx, ada_w, ada_b, norm_mix_g, norm_ffn_g, ev_w_in, ev_w_out, lru_conv_w,
                    lru_conv_b, lru_w_r, lru_b_r, lru_w_i, lru_b_i, lru_lambda, cf_dw_w, cf_dw_b, cf_ln_g,
                    cf_ln_b, at_w_qkv, at_w_out, at_q_norm_g, at_k_norm_g, router_w, router_bias, exp_w_gate,
                    exp_w_up, exp_w_down, final_norm_g)
```

```python
import functools
import math

import jax
import jax.numpy as jnp
from jax import lax
from jax.experimental import pallas as pl
from jax.experimental.pallas import tpu as pltpu

F32 = jnp.float32
BF16 = jnp.bfloat16

D_MODEL = 1024
N_MOD = 6
NORM_EPS = 1e-6
GRID_W = 64

D_LRU = 512
LRU_HEADS = 8
LRU_HEAD_DIM = D_LRU // LRU_HEADS
LRU_CONV_W = 4
LRU_C = 8.0
D_CONV = 512
CONF_WIDTH = 31
CONF_PAD = 16

HEAD_DIM = 128
N_Q_HEADS = 8
N_KV_HEADS = 2
GQA_GROUP = N_Q_HEADS // N_KV_HEADS
ROPE_THETA = 10000.0

N_EXPERTS = 16
N_GROUPS = 4
EXPERTS_PER_GROUP = 4
D_EXPERT = 512

LANES = 128
SUBLANES = 8
TM = 512
TMS = 256
PAIRS_PER_GROUP = 6
N_CLASSES = N_GROUPS * PAIRS_PER_GROUP
CLASS_ROWS = 32
ISSUE_UNROLL = 8
TD = 1024
ROW_W = D_MODEL + LANES
CT = 128
TQ = 512
TK = 1024
CONV_CHUNK = 128
LRU_CHUNK = 256
CARRY_UNROLL = 4
VMEM_LIMIT = 56 * 1024 * 1024


def _cparams(sem):
    return pltpu.CompilerParams(dimension_semantics=sem, vmem_limit_bytes=VMEM_LIMIT)


def _rms(x, g):
    return x * lax.rsqrt(jnp.mean(x * x, axis=-1, keepdims=True) + NORM_EPS) * g


def _split_bf16(a):
    hi = a.astype(BF16)
    lo = (a - hi.astype(F32)).astype(BF16)
    return hi, lo


def _dot(a, b):
    return jnp.dot(a, b, preferred_element_type=F32)


def _dot3(a_hi, a_lo, b_hi, b_lo):
    return _dot(a_hi, b_hi) + (_dot(a_lo, b_hi) + _dot(a_hi, b_lo))


def _mod_row(i, n_lat_tiles, tiles_per_batch, nb):
    return jnp.where(i < n_lat_tiles, i // tiles_per_batch, nb)


def _adaln_kernel(c_ref, w_ref, b_ref, o_ref):
    cv = c_ref[...]
    s = cv * jax.nn.sigmoid(cv)
    s_hi, s_lo = _split_bf16(s)
    w_hi, w_lo = _split_bf16(w_ref[...])
    o_ref[...] = _dot3(s_hi, s_lo, w_hi, w_lo) + b_ref[...]


def _adaln(cvec, ada_w, ada_b, interpret):
    depth = ada_w.shape[0]
    n_out = ada_w.shape[2]
    tn = D_MODEL
    return pl.pallas_call(
        _adaln_kernel,
        out_shape=jax.ShapeDtypeStruct((depth, SUBLANES, n_out), F32),
        grid=(depth, n_out // tn),
        in_specs=[
            pl.BlockSpec((SUBLANES, D_MODEL), lambda l, j: (0, 0)),
            pl.BlockSpec((None, D_MODEL, tn), lambda l, j: (l, 0, j)),
            pl.BlockSpec((None, 1, tn), lambda l, j: (l, 0, j)),
        ],
        out_specs=pl.BlockSpec((None, SUBLANES, tn), lambda l, j: (l, 0, j)),
        compiler_params=_cparams(("arbitrary", "arbitrary")),
        name="adaln_mod",
        interpret=interpret,
    )(cvec, ada_w, ada_b.reshape(depth, 1, n_out))


def _even_in_kernel(x_ref, mod_ref, g_ref, w_ref, o_ref, *, n_lat_tiles, tiles_per_batch, nb):
    i = pl.program_id(0)
    row = _mod_row(i, n_lat_tiles, tiles_per_batch, nb)
    sh = mod_ref[pl.ds(row, 1), 0:D_MODEL]
    sc = mod_ref[pl.ds(row, 1), D_MODEL:2 * D_MODEL]
    h = _rms(x_ref[...], g_ref[...]) * (1.0 + sc) + sh
    u = _dot(h.astype(BF16), w_ref[...])
    o_ref[:, 0:D_LRU] = u[:, 0:D_LRU]
    o_ref[:, D_LRU:2 * D_LRU] = jax.nn.gelu(u[:, D_LRU:2 * D_LRU])
    val = u[:, 2 * D_LRU:2 * D_LRU + D_CONV]
    gate = u[:, 2 * D_LRU + D_CONV:]
    o_ref[:, 2 * D_LRU:] = val * jax.nn.sigmoid(gate)


def _even_in(x, mod_l, g, w_in, dims, interpret):
    nb, s_len, c_len = dims
    t = x.shape[0]
    n_out = 2 * D_LRU + D_CONV
    kern = functools.partial(_even_in_kernel, n_lat_tiles=nb * s_len // TM,
                             tiles_per_batch=s_len // TM, nb=nb)
    return pl.pallas_call(
        kern,
        out_shape=jax.ShapeDtypeStruct((t, n_out), F32),
        grid=(t // TM,),
        in_specs=[
            pl.BlockSpec((TM, D_MODEL), lambda i: (i, 0)),
            pl.BlockSpec(mod_l.shape, lambda i: (0, 0)),
            pl.BlockSpec((1, D_MODEL), lambda i: (0, 0)),
            pl.BlockSpec(w_in.shape, lambda i: (0, 0)),
        ],
        out_specs=pl.BlockSpec((TM, n_out), lambda i: (i, 0)),
        compiler_params=_cparams(("arbitrary",)),
        name="even_in",
        interpret=interpret,
    )(x, mod_l, g.reshape(1, D_MODEL), w_in)


def _block_scan(a, b, reverse):
    n, width = a.shape
    a = a.reshape(n // SUBLANES, SUBLANES, width)
    b = b.reshape(n // SUBLANES, SUBLANES, width)
    tmod = lax.broadcasted_iota(jnp.int32, a.shape, 1)
    d = 1
    while d < SUBLANES:
        shift = SUBLANES - d if reverse else d
        keep = (tmod < SUBLANES - d) if reverse else (tmod >= d)
        a_n = pltpu.roll(a, shift, axis=1)
        b_n = pltpu.roll(b, shift, axis=1)
        b = a * jnp.where(keep, b_n, 0.0) + b
        a = a * jnp.where(keep, a_n, 1.0)
        d *= 2
    return a.reshape(n, width), b.reshape(n, width)


def _lru_coeffs(u, ub, wr, br, wi, bi, nsp):
    r = jax.nn.sigmoid(_dot(ub, wr) + br)
    i = jax.nn.sigmoid(_dot(ub, wi) + bi)
    log_a = (-LRU_C) * r * nsp
    a = jnp.exp(log_a)
    b = jnp.sqrt(1.0 - a * a) * (i * u)
    return a, b


def _seq_kernel(xl_ref, xc_ref, gl_ref, gc_ref, vl_ref, vc_ref,
                cw_ref, cb_ref, wr_ref, br_ref, wi_ref, bi_ref, sp_ref, dw_ref, db_ref,
                ol_ref, oc_ref,
                af_ref, bf_ref, ar_ref, br_sc_ref, hf_ref, hr_ref, xp_ref, ph_ref, *, n_lru_tiles):
    c = pl.program_id(1)
    s_len = xl_ref.shape[0]
    c_len = xc_ref.shape[0]

    @pl.when(c < n_lru_tiles)
    def _lru():
        cw = cw_ref[...]
        cb = cb_ref[...]
        nsp = [jax.nn.softplus(-sp_ref[d]) for d in range(2)]

        def coeffs(x_ref, n):
            zpad = jnp.zeros((SUBLANES, CT), F32)
            xp_ref[0:SUBLANES] = zpad
            xp_ref[SUBLANES:SUBLANES + n] = x_ref[...]
            xp_ref[SUBLANES + n:2 * SUBLANES + n] = zpad
            ch = min(LRU_CHUNK, n)

            def body(j, _):
                t0 = pl.multiple_of(j * ch, ch)
                win = xp_ref[pl.ds(t0, ch + 2 * SUBLANES)]
                off = SUBLANES - LRU_CONV_W // 2
                u = cb
                for k in range(LRU_CONV_W):
                    u = u + cw[k:k + 1] * win[off + k:off + k + ch]
                ub = u.astype(BF16)
                a_f, b_f = _lru_coeffs(u, ub, wr_ref[0], br_ref[0], wi_ref[0], bi_ref[0], nsp[0])
                a_f, b_f = _block_scan(a_f, b_f, False)
                af_ref[pl.ds(t0, ch)] = a_f
                bf_ref[pl.ds(t0, ch)] = b_f
                a_r, b_r = _lru_coeffs(u, ub, wr_ref[1], br_ref[1], wi_ref[1], bi_ref[1], nsp[1])
                a_r, b_r = _block_scan(a_r, b_r, True)
                ar_ref[pl.ds(t0, ch)] = a_r
                br_sc_ref[pl.ds(t0, ch)] = b_r
                return 0

            lax.fori_loop(0, n // ch, body, 0)

        def carry_pass(n, h0f, h0r):
            nblk = n // SUBLANES

            def body(j, carry):
                hf, hr = carry
                for u in range(CARRY_UNROLL):
                    k = j * CARRY_UNROLL + u
                    kf = pl.multiple_of(k * SUBLANES, SUBLANES)
                    kr = pl.multiple_of((nblk - 1 - k) * SUBLANES, SUBLANES)
                    yf = bf_ref[pl.ds(kf, SUBLANES)] + af_ref[pl.ds(kf, SUBLANES)] * hf
                    yr = br_sc_ref[pl.ds(kr, SUBLANES)] + ar_ref[pl.ds(kr, SUBLANES)] * hr
                    hf_ref[pl.ds(kf, SUBLANES)] = yf
                    hr_ref[pl.ds(kr, SUBLANES)] = yr
                    hf, hr = yf[SUBLANES - 1:SUBLANES], yr[0:1]
                return hf, hr

            return lax.fori_loop(0, nblk // CARRY_UNROLL, body, (h0f, h0r))

        zero = jnp.zeros((1, CT), F32)
        coeffs(xc_ref, c_len)
        hcf, hcr = carry_pass(c_len, zero, zero)
        oc_ref[...] = (hf_ref[0:c_len] + hr_ref[0:c_len]) * gc_ref[...]
        coeffs(xl_ref, s_len)
        carry_pass(s_len, hcf, hcr)
        ol_ref[...] = (hf_ref[0:s_len] + hr_ref[0:s_len]) * gl_ref[...]

    @pl.when(c >= n_lru_tiles)
    def _conv():
        dw = dw_ref[...]
        db = db_ref[...]
        zpad = jnp.zeros((CONF_PAD, CT), F32)

        def conv(v_ref, o_ref, n):
            xp_ref[0:CONF_PAD] = zpad
            xp_ref[CONF_PAD:CONF_PAD + n] = v_ref[...]
            xp_ref[CONF_PAD + n:2 * CONF_PAD + n] = zpad

            def body(j, _):
                t0 = pl.multiple_of(j * CONV_CHUNK, CONV_CHUNK)
                win = xp_ref[pl.ds(t0, CONV_CHUNK + 2 * CONF_PAD)]
                acc = jnp.broadcast_to(db, (CONV_CHUNK, CT))
                off = CONF_PAD - CONF_WIDTH // 2
                span = CONV_CHUNK + 2 * CONF_PAD - SUBLANES
                for r in range(SUBLANES):
                    ph_ref[r] = win[r:r + span]
                for k in range(CONF_WIDTH):
                    r = (off + k) % SUBLANES
                    q = (off + k) // SUBLANES * SUBLANES
                    acc = acc + dw[k:k + 1] * ph_ref[r, q:q + CONV_CHUNK]
                o_ref[pl.ds(t0, CONV_CHUNK)] = acc
                return 0

            lax.fori_loop(0, n // CONV_CHUNK, body, 0)

        conv(vc_ref, oc_ref, c_len)
        conv(vl_ref, ol_ref, s_len)


def _seq_mix(u_all, p, dims, interpret):
    nb, s_len, c_len = dims
    n_lru = D_LRU // CT
    n_cv = D_CONV // CT
    lat_blocks = nb * s_len // c_len

    def lru_c(c):
        return jnp.minimum(c, n_lru - 1)

    def cv_c(c):
        return jnp.maximum(c - n_lru, 0)

    in_specs = [
        pl.BlockSpec((s_len, CT), lambda b, c: (b, lru_c(c))),
        pl.BlockSpec((c_len, CT), lambda b, c: (lat_blocks + b, lru_c(c))),
        pl.BlockSpec((s_len, CT), lambda b, c: (b, n_lru + lru_c(c))),
        pl.BlockSpec((c_len, CT), lambda b, c: (lat_blocks + b, n_lru + lru_c(c))),
        pl.BlockSpec((s_len, CT), lambda b, c: (b, 2 * n_lru + cv_c(c))),
        pl.BlockSpec((c_len, CT), lambda b, c: (lat_blocks + b, 2 * n_lru + cv_c(c))),
        pl.BlockSpec((LRU_CONV_W, CT), lambda b, c: (0, lru_c(c))),
        pl.BlockSpec((1, CT), lambda b, c: (0, lru_c(c))),
        pl.BlockSpec((2, None, CT, CT), lambda b, c: (0, lru_c(c), 0, 0)),
        pl.BlockSpec((2, 1, CT), lambda b, c: (0, 0, lru_c(c))),
        pl.BlockSpec((2, None, CT, CT), lambda b, c: (0, lru_c(c), 0, 0)),
        pl.BlockSpec((2, 1, CT), lambda b, c: (0, 0, lru_c(c))),
        pl.BlockSpec((2, 1, CT), lambda b, c: (0, 0, lru_c(c))),
        pl.BlockSpec((CONF_WIDTH, CT), lambda b, c: (0, cv_c(c))),
        pl.BlockSpec((1, CT), lambda b, c: (0, cv_c(c))),
    ]
    out_specs = [
        pl.BlockSpec((s_len, CT), lambda b, c: (b, c)),
        pl.BlockSpec((c_len, CT), lambda b, c: (b, c)),
    ]
    scratch = [pltpu.VMEM((s_len, CT), F32) for _ in range(6)]
    scratch.append(pltpu.VMEM((s_len + 2 * CONF_PAD, CT), F32))
    scratch.append(pltpu.VMEM((SUBLANES, CONV_CHUNK + 2 * CONF_PAD - SUBLANES, CT), F32))
    return pl.pallas_call(
        functools.partial(_seq_kernel, n_lru_tiles=n_lru),
        out_shape=[jax.ShapeDtypeStruct((nb * s_len, D_LRU + D_CONV), F32),
                   jax.ShapeDtypeStruct((nb * c_len, D_LRU + D_CONV), F32)],
        grid=(nb, n_lru + n_cv),
        in_specs=in_specs,
        out_specs=out_specs,
        scratch_shapes=scratch,
        compiler_params=_cparams(("arbitrary", "arbitrary")),
        name="seq_mix",
        interpret=interpret,
    )(u_all, u_all, u_all, u_all, u_all, u_all,
      p["conv_w"], p["conv_b"], p["w_r"], p["b_r"], p["w_i"], p["b_i"], p["lam"], p["dw_w"], p["dw_b"])


def _swap32(x):
    lane = lax.broadcasted_iota(jnp.int32, x.shape, 1)
    return jnp.where((lane & 32) == 0, pltpu.roll(x, HEAD_DIM - 32, axis=1), pltpu.roll(x, 32, axis=1))


def _odd_in_kernel(x_ref, mod_ref, g_ref, w_ref, qg_ref, kg_ref, cos_ref, sin_ref, o_ref,
                   *, n_lat_tiles, tiles_per_batch, nb):
    i = pl.program_id(0)
    row = _mod_row(i, n_lat_tiles, tiles_per_batch, nb)
    sh = mod_ref[pl.ds(row, 1), 0:D_MODEL]
    sc = mod_ref[pl.ds(row, 1), D_MODEL:2 * D_MODEL]
    h = _rms(x_ref[...], g_ref[...]) * (1.0 + sc) + sh
    u = _dot(h.astype(BF16), w_ref[...])
    is_lat = i < n_lat_tiles
    cos = jnp.where(is_lat, cos_ref[...], 1.0)
    sin = jnp.where(is_lat, sin_ref[...], 0.0)
    q_scale = math.log2(math.e) / math.sqrt(HEAD_DIM)
    for hh in range(N_Q_HEADS + N_KV_HEADS):
        xh = u[:, hh * HEAD_DIM:(hh + 1) * HEAD_DIM]
        gain = qg_ref[...] if hh < N_Q_HEADS else kg_ref[...]
        xn = _rms(xh, gain)
        y = xn * cos + _swap32(xn) * sin
        if hh < N_Q_HEADS:
            y = y * q_scale
        o_ref[:, hh * HEAD_DIM:(hh + 1) * HEAD_DIM] = y.astype(BF16)
    v0 = (N_Q_HEADS + N_KV_HEADS) * HEAD_DIM
    o_ref[:, v0:] = u[:, v0:].astype(BF16)


def _odd_in(x, mod_l, g, w_qkv, q_g, k_g, rope_cos, rope_sin, dims, interpret):
    nb, s_len, c_len = dims
    t = x.shape[0]
    n_out = w_qkv.shape[1]
    tpb = s_len // TM
    n_lat = nb * tpb
    kern = functools.partial(_odd_in_kernel, n_lat_tiles=n_lat, tiles_per_batch=tpb, nb=nb)

    def pos_block(i):
        return jnp.where(i < n_lat, i % tpb, 0)

    return pl.pallas_call(
        kern,
        out_shape=jax.ShapeDtypeStruct((t, n_out), BF16),
        grid=(t // TM,),
        in_specs=[
            pl.BlockSpec((TM, D_MODEL), lambda i: (i, 0)),
            pl.BlockSpec(mod_l.shape, lambda i: (0, 0)),
            pl.BlockSpec((1, D_MODEL), lambda i: (0, 0)),
            pl.BlockSpec(w_qkv.shape, lambda i: (0, 0)),
            pl.BlockSpec((1, HEAD_DIM), lambda i: (0, 0)),
            pl.BlockSpec((1, HEAD_DIM), lambda i: (0, 0)),
            pl.BlockSpec((TM, HEAD_DIM), lambda i: (pos_block(i), 0)),
            pl.BlockSpec((TM, HEAD_DIM), lambda i: (pos_block(i), 0)),
        ],
        out_specs=pl.BlockSpec((TM, n_out), lambda i: (i, 0)),
        compiler_params=_cparams(("arbitrary",)),
        name="odd_in",
        interpret=interpret,
    )(x, mod_l, g.reshape(1, D_MODEL), w_qkv, q_g.reshape(1, HEAD_DIM), k_g.reshape(1, HEAD_DIM),
      rope_cos, rope_sin)


def _attn_kernel(*refs, n_seg):
    q_ref = refs[0]
    k_refs = refs[1:1 + n_seg]
    v_refs = refs[1 + n_seg:1 + 2 * n_seg]
    o_ref = refs[1 + 2 * n_seg]
    tq = q_ref.shape[0]
    chunks = []
    for k_ref, v_ref in zip(k_refs, v_refs):
        n = k_ref.shape[0]
        step = min(TK, n)
        for s0 in range(0, n, step):
            chunks.append((k_ref, v_ref, s0, step))
    for g in range(GQA_GROUP):
        q = q_ref[:, g * HEAD_DIM:(g + 1) * HEAD_DIM]
        m = jnp.full((tq, 1), -jnp.inf, F32)
        acc = jnp.zeros((tq, 2 * HEAD_DIM), F32)
        for k_ref, v_ref, s0, step in chunks:
            k = k_ref[s0:s0 + step, :]
            v1 = jnp.concatenate([v_ref[s0:s0 + step, :], jnp.ones((step, HEAD_DIM), BF16)], axis=1)
            s = lax.dot_general(q, k, (((1,), (1,)), ((), ())), preferred_element_type=F32)
            m_new = jnp.maximum(m, jnp.max(s, axis=-1, keepdims=True))
            p = jnp.exp2(s - m_new)
            acc = jnp.exp2(m - m_new) * acc + _dot(p.astype(BF16), v1)
            m = m_new
        o_ref[:, g * HEAD_DIM:(g + 1) * HEAD_DIM] = (acc[:, 0:HEAD_DIM] / acc[:, HEAD_DIM:HEAD_DIM + 1]).astype(BF16)


def _attention(qkv, dims, latent, interpret):
    nb, s_len, c_len = dims
    k_col = N_Q_HEADS
    v_col = N_Q_HEADS + N_KV_HEADS
    lat_blocks = nb * s_len // c_len
    ctx_k = pl.BlockSpec((c_len, HEAD_DIM), lambda b, h, qi: (lat_blocks + b, k_col + h))
    ctx_v = pl.BlockSpec((c_len, HEAD_DIM), lambda b, h, qi: (lat_blocks + b, v_col + h))
    if latent:
        tq = TQ
        nq = s_len // tq
        q_spec = pl.BlockSpec((tq, GQA_GROUP * HEAD_DIM), lambda b, h, qi: (b * nq + qi, h))
        k_specs = [pl.BlockSpec((s_len, HEAD_DIM), lambda b, h, qi: (b, k_col + h)), ctx_k]
        v_specs = [pl.BlockSpec((s_len, HEAD_DIM), lambda b, h, qi: (b, v_col + h)), ctx_v]
        out_rows = nb * s_len
        o_spec = pl.BlockSpec((tq, GQA_GROUP * HEAD_DIM), lambda b, h, qi: (b * nq + qi, h))
    else:
        tq = c_len
        nq = 1
        q_spec = pl.BlockSpec((tq, GQA_GROUP * HEAD_DIM), lambda b, h, qi: (lat_blocks + b, h))
        k_specs = [ctx_k]
        v_specs = [ctx_v]
        out_rows = nb * c_len
        o_spec = pl.BlockSpec((tq, GQA_GROUP * HEAD_DIM), lambda b, h, qi: (b, h))
    n_seg = len(k_specs)
    return pl.pallas_call(
        functools.partial(_attn_kernel, n_seg=n_seg),
        out_shape=jax.ShapeDtypeStruct((out_rows, N_Q_HEADS * HEAD_DIM), BF16),
        grid=(nb, N_KV_HEADS, nq),
        in_specs=[q_spec] + k_specs + v_specs,
        out_specs=o_spec,
        compiler_params=_cparams(("arbitrary", "arbitrary", "arbitrary")),
        name="attn_lat" if latent else "attn_ctx",
        interpret=interpret,
    )(*([qkv] * (1 + 2 * n_seg)))


def _route(logits_t, bias):
    score = [jax.nn.sigmoid(logits_t[e:e + 1]) for e in range(N_EXPERTS)]
    sel = [score[e] + bias[e] for e in range(N_EXPERTS)]
    grp = []
    for g in range(N_GROUPS):
        a, b, c, d = sel[4 * g:4 * g + 4]
        hi1, lo1 = jnp.maximum(a, b), jnp.minimum(a, b)
        hi2, lo2 = jnp.maximum(c, d), jnp.minimum(c, d)
        top1 = jnp.maximum(hi1, hi2)
        top2 = jnp.maximum(jnp.minimum(hi1, hi2), jnp.maximum(lo1, lo2))
        grp.append(top1 + top2)
    best = jnp.zeros_like(grp[0], dtype=jnp.int32)
    best_s = grp[0]
    for g in range(1, N_GROUPS):
        better = grp[g] > best_s
        best = jnp.where(better, g, best)
        best_s = jnp.where(better, grp[g], best_s)
    chosen = []
    for e in range(N_EXPERTS):
        g = e // EXPERTS_PER_GROUP
        beaten = jnp.zeros_like(best)
        for k in range(4 * g, 4 * g + 4):
            if k == e:
                continue
            beats = (sel[k] >= sel[e]) if k < e else (sel[k] > sel[e])
            beaten = beaten + jnp.where(beats, 1, 0)
        chosen.append((beaten < 2) & (best == g))
    wsum = jnp.zeros_like(score[0])
    for e in range(N_EXPERTS):
        wsum = wsum + jnp.where(chosen[e], score[e], 0.0)
    picked, gate = [], []
    for j in range(EXPERTS_PER_GROUP):
        cj = jnp.zeros_like(best)
        gj = jnp.zeros_like(wsum)
        for g in range(N_GROUPS):
            e = EXPERTS_PER_GROUP * g + j
            cj = jnp.where(chosen[e], 1, cj)
            gj = jnp.where(best == g, score[e], gj)
        picked.append(cj > 0)
        gate.append(gj / wsum)
    j_lo = jnp.where(picked[0], 0, jnp.where(picked[1], 1, 2))
    j_hi = jnp.where(picked[3], 3, jnp.where(picked[2], 2, 1))
    pair = jnp.where(j_lo == 0, j_hi - 1, jnp.where(j_lo == 1, j_hi + 1, PAIRS_PER_GROUP - 1))
    gate_lo = jnp.where(j_lo == 0, gate[0], jnp.where(j_lo == 1, gate[1], gate[2]))
    gate_hi = jnp.where(j_hi == 3, gate[3], jnp.where(j_hi == 2, gate[2], gate[1]))
    return best * PAIRS_PER_GROUP + pair, gate_lo, gate_hi


def _mix_out_kernel(yl_ref, yc_ref, x_ref, mod_ref, w_ref, lng_ref, lnb_ref, nfg_ref, rwh_ref, rwl_ref, rb_ref,
                    xo_ref, hrow_ref, pos_ref, cnt_ref, base_ref,
                    *, even, n_lat_tiles, tiles_per_batch, nb, cap):
    i = pl.program_id(0)

    @pl.when(i == 0)
    def _init():
        base_ref[...] = jnp.zeros_like(base_ref)

    row = _mod_row(i, n_lat_tiles, tiles_per_batch, nb)
    g1 = mod_ref[pl.ds(row, 1), 2 * D_MODEL:3 * D_MODEL]
    sh2 = mod_ref[pl.ds(row, 1), 3 * D_MODEL:4 * D_MODEL]
    sc2 = mod_ref[pl.ds(row, 1), 4 * D_MODEL:5 * D_MODEL]
    is_lat = i < n_lat_tiles
    y = jnp.where(is_lat, yl_ref[...], yc_ref[...])
    if even:
        ya = y[:, 0:D_LRU]
        vc = y[:, D_LRU:]
        mu = jnp.mean(vc, axis=-1, keepdims=True)
        xc = vc - mu
        var = jnp.mean(xc * xc, axis=-1, keepdims=True)
        ln = xc * lax.rsqrt(var + NORM_EPS) * lng_ref[...] + lnb_ref[...]
        yb = ln * jax.nn.sigmoid(ln)
        out = _dot(ya.astype(BF16), w_ref[0:D_LRU, :]) + _dot(yb.astype(BF16), w_ref[D_LRU:, :])
    else:
        out = _dot(y, w_ref[...])
    x_new = x_ref[...] + g1 * out
    xo_ref[...] = x_new
    h2 = _rms(x_new, nfg_ref[...]) * (1.0 + sc2) + sh2
    hrow_ref[:, 0:D_MODEL] = h2
    h_hi, h_lo = _split_bf16(h2)
    logits = _dot3(h_hi, h_lo, rwh_ref[...], rwl_ref[...])
    logits_t = jnp.transpose(logits)[0:N_EXPERTS]
    cls, gate_lo, gate_hi = _route(logits_t, [rb_ref[e] for e in range(N_EXPERTS)])
    tm = logits.shape[0]

    sub = lax.broadcasted_iota(jnp.int32, (CLASS_ROWS, tm), 0)
    onehot = jnp.where(sub == cls, 1.0, 0.0)
    earlier = (lax.broadcasted_iota(jnp.int32, (tm, tm), 0) < lax.broadcasted_iota(jnp.int32, (tm, tm), 1))
    before = _dot(onehot.astype(BF16), jnp.where(earlier, 1.0, 0.0).astype(BF16))
    rank = jnp.sum(onehot * (before + base_ref[:, 0:1]), axis=0, keepdims=True)
    pos_ref[...] = cls * cap + rank.astype(jnp.int32)
    base_ref[...] = base_ref[...] + jnp.sum(onehot, axis=1, keepdims=True)
    cnt_ref[...] = base_ref[...]

    sub8 = lax.broadcasted_iota(jnp.int32, (SUBLANES, tm), 0)
    extras_t = jnp.where(sub8 == 0, gate_lo, jnp.where(sub8 == 1, gate_hi, 0.0))
    extras_t = jnp.concatenate([extras_t, jnp.zeros((LANES - SUBLANES, tm), F32)], axis=0)
    hrow_ref[:, D_MODEL:] = jnp.transpose(extras_t)


def _mix_out(y_lat, y_ctx, x, mod_l, w_out, ln_g, ln_b, nf_g, rw_hi, rw_lo, router_bias, dims, n_tiles, cap, even,
             interpret):
    nb, s_len, c_len = dims
    tpb = s_len // TM
    n_lat = nb * tpb
    n_ctx = max(nb * c_len // TM, 1)
    rows = n_tiles * TM
    kern = functools.partial(_mix_out_kernel, even=even, n_lat_tiles=n_lat, tiles_per_batch=tpb, nb=nb, cap=cap)
    vec = lambda n: pl.BlockSpec((1, n), lambda i: (0, 0))
    return pl.pallas_call(
        kern,
        out_shape=[jax.ShapeDtypeStruct((rows, D_MODEL), F32),
                   jax.ShapeDtypeStruct((rows, ROW_W), F32),
                   jax.ShapeDtypeStruct((n_tiles, 1, TM), jnp.int32),
                   jax.ShapeDtypeStruct((CLASS_ROWS, LANES), F32)],
        grid=(n_tiles,),
        in_specs=[
            pl.BlockSpec((TM, D_MODEL), lambda i: (jnp.minimum(i, n_lat - 1), 0)),
            pl.BlockSpec((TM, D_MODEL), lambda i: (jnp.clip(i - n_lat, 0, n_ctx - 1), 0)),
            pl.BlockSpec((TM, D_MODEL), lambda i: (i, 0)),
            pl.BlockSpec(mod_l.shape, lambda i: (0, 0)),
            pl.BlockSpec(w_out.shape, lambda i: (0, 0)),
            vec(D_CONV), vec(D_CONV), vec(D_MODEL),
            pl.BlockSpec(rw_hi.shape, lambda i: (0, 0)),
            pl.BlockSpec(rw_lo.shape, lambda i: (0, 0)),
            pl.BlockSpec(memory_space=pltpu.SMEM),
        ],
        out_specs=[
            pl.BlockSpec((TM, D_MODEL), lambda i: (i, 0)),
            pl.BlockSpec((TM, ROW_W), lambda i: (i, 0)),
            pl.BlockSpec((None, 1, TM), lambda i: (i, 0, 0)),
            pl.BlockSpec((CLASS_ROWS, LANES), lambda i: (0, 0)),
        ],
        scratch_shapes=[pltpu.VMEM((CLASS_ROWS, LANES), F32)],
        compiler_params=_cparams(("arbitrary",)),
        name="mix_out_even" if even else "mix_out_odd",
        interpret=interpret,
    )(y_lat, y_ctx, x, mod_l, w_out, ln_g.reshape(1, D_CONV), ln_b.reshape(1, D_CONV),
      nf_g.reshape(1, D_MODEL), rw_hi, rw_lo, router_bias)


def _row_copy(src_ref, src_row, dst_ref, dst_row, sem):
    return pltpu.make_async_copy(src_ref.at[pl.ds(src_row, 1)], dst_ref.at[pl.ds(dst_row, 1)], sem)


def _dispatch_kernel(pos_ref, fill_ref, nact_ref, h_ref, xs_ref, zero_ref, row_sem, fill_sem, tail_sem, *, n_slots):
    i = pl.program_id(0)
    tm = h_ref.shape[0]

    @pl.when(i == 0)
    def _fill():
        zero_ref[...] = jnp.zeros_like(zero_ref)
        def fill(tile, sem):
            return pltpu.make_async_copy(zero_ref, xs_ref.at[pl.ds(pl.multiple_of(tile * TMS, TMS), TMS)], sem)

        for phase in ("start", "wait"):
            for c in range(N_CLASSES):
                @pl.when(fill_ref[c] >= 0)
                def _last_tile():
                    getattr(fill(fill_ref[c], fill_sem.at[c]), phase)()

                @pl.when(n_slots - 1 - c >= nact_ref[0])
                def _unused_tile():
                    getattr(fill(n_slots - 1 - c, tail_sem.at[c]), phase)()

    def issue(j, _):
        for k in range(ISSUE_UNROLL):
            r = j * ISSUE_UNROLL + k
            _row_copy(h_ref, r, xs_ref, pos_ref[i * tm + r], row_sem).start()
        return 0

    lax.fori_loop(0, tm // ISSUE_UNROLL, issue, 0)
    pltpu.make_async_copy(h_ref, xs_ref.at[pl.ds(0, tm)], row_sem).wait()


def _dispatch(hrow, pos_flat, fill_tile, n_active, n_slots, td, interpret):
    rows = hrow.shape[0]
    return pl.pallas_call(
        functools.partial(_dispatch_kernel, n_slots=n_slots),
        out_shape=jax.ShapeDtypeStruct((n_slots * TMS, ROW_W), F32),
        grid_spec=pltpu.PrefetchScalarGridSpec(
            num_scalar_prefetch=3,
            grid=(rows // td,),
            in_specs=[pl.BlockSpec((td, ROW_W), lambda i, pos, fill, nact: (i, 0))],
            out_specs=pl.BlockSpec(memory_space=pl.ANY),
            scratch_shapes=[pltpu.VMEM((TMS, ROW_W), F32), pltpu.SemaphoreType.DMA,
                            pltpu.SemaphoreType.DMA((N_CLASSES,)), pltpu.SemaphoreType.DMA((N_CLASSES,))],
        ),
        compiler_params=_cparams(("arbitrary",)),
        name="moe_dispatch",
        interpret=interpret,
    )(pos_flat, fill_tile, n_active, hrow)


def _experts_kernel(elo_ref, ehi_ref, nact_ref, xs_ref, wg0_ref, wu0_ref, wd0_ref, wg1_ref, wu1_ref, wd1_ref,
                    ys_ref):
    i = pl.program_id(0)

    @pl.when(i < nact_ref[0])
    def _active():
        h = xs_ref[:, 0:D_MODEL].astype(BF16)
        out = None
        for lane, (wg_ref, wu_ref, wd_ref) in enumerate(((wg0_ref, wu0_ref, wd0_ref),
                                                         (wg1_ref, wu1_ref, wd1_ref))):
            a = _dot(h, wg_ref[...])
            u = _dot(h, wu_ref[...])
            hid = (a * jax.nn.sigmoid(a)) * u * xs_ref[:, D_MODEL + lane:D_MODEL + lane + 1]
            y = _dot(hid.astype(BF16), wd_ref[...])
            out = y if out is None else out + y
        ys_ref[...] = out

    @pl.when(i >= nact_ref[0])
    def _unused():
        ys_ref[...] = jnp.zeros_like(ys_ref)


def _experts(xs, tile_lo, tile_hi, n_active, wg, wu, wd, layer, interpret):
    n_slots = tile_lo.shape[0]
    lo_idx = lambda i, lo, hi, nact: (layer, lo[i], 0, 0)
    hi_idx = lambda i, lo, hi, nact: (layer, hi[i], 0, 0)
    w_in = (None, None, D_MODEL, D_EXPERT)
    w_out = (None, None, D_EXPERT, D_MODEL)
    return pl.pallas_call(
        _experts_kernel,
        out_shape=jax.ShapeDtypeStruct((n_slots * TMS, D_MODEL), F32),
        grid_spec=pltpu.PrefetchScalarGridSpec(
            num_scalar_prefetch=3,
            grid=(n_slots,),
            in_specs=[
                pl.BlockSpec((TMS, ROW_W), lambda i, lo, hi, nact: (jnp.minimum(i, nact[0] - 1), 0)),
                pl.BlockSpec(w_in, lo_idx), pl.BlockSpec(w_in, lo_idx), pl.BlockSpec(w_out, lo_idx),
                pl.BlockSpec(w_in, hi_idx), pl.BlockSpec(w_in, hi_idx), pl.BlockSpec(w_out, hi_idx),
            ],
            out_specs=pl.BlockSpec((TMS, D_MODEL), lambda i, lo, hi, nact: (i, 0)),
        ),
        compiler_params=_cparams(("arbitrary",)),
        name="moe_experts",
        interpret=interpret,
    )(tile_lo, tile_hi, n_active, xs, wg, wu, wd, wg, wu, wd)


def _combine_kernel(pos_ref, ys_ref, x_ref, mod_ref, fg_ref, o_ref, ybuf_ref, sem,
                    *, final, n_lat_tiles, tiles_per_batch, nb):
    i = pl.program_id(0)
    tm = x_ref.shape[0]

    def issue(j, _):
        for k in range(ISSUE_UNROLL):
            r = j * ISSUE_UNROLL + k
            _row_copy(ys_ref, pos_ref[i * tm + r], ybuf_ref, r, sem).start()
        return 0

    lax.fori_loop(0, tm // ISSUE_UNROLL, issue, 0)
    pltpu.make_async_copy(ys_ref.at[pl.ds(0, tm)], ybuf_ref, sem).wait()
    row = _mod_row(i, n_lat_tiles, tiles_per_batch, nb)
    g2 = mod_ref[pl.ds(row, 1), 5 * D_MODEL:6 * D_MODEL]
    x_new = x_ref[...] + g2 * ybuf_ref[...]
    if final:
        x_new = _rms(x_new, fg_ref[...])
    o_ref[...] = x_new


def _combine(ys, pos_flat, x, mod_l, final_g, dims, td, final, interpret):
    nb, s_len, c_len = dims
    rows = x.shape[0]
    tpb = s_len // td
    kern = functools.partial(_combine_kernel, final=final, n_lat_tiles=nb * tpb, tiles_per_batch=tpb, nb=nb)
    return pl.pallas_call(
        kern,
        out_shape=jax.ShapeDtypeStruct((rows, D_MODEL), F32),
        grid_spec=pltpu.PrefetchScalarGridSpec(
            num_scalar_prefetch=1,
            grid=(rows // td,),
            in_specs=[
                pl.BlockSpec(memory_space=pl.ANY),
                pl.BlockSpec((td, D_MODEL), lambda i, pos: (i, 0)),
                pl.BlockSpec(mod_l.shape, lambda i, pos: (0, 0)),
                pl.BlockSpec((1, D_MODEL), lambda i, pos: (0, 0)),
            ],
            out_specs=pl.BlockSpec((td, D_MODEL), lambda i, pos: (i, 0)),
            scratch_shapes=[pltpu.VMEM((td, D_MODEL), F32), pltpu.SemaphoreType.DMA],
        ),
        compiler_params=_cparams(("arbitrary",)),
        name="moe_combine_final" if final else "moe_combine",
        interpret=interpret,
    )(pos_flat, ys, x, mod_l, final_g.reshape(1, D_MODEL))


def _sorted_layout(pos, counts, cap, n_slots):
    cnt = counts[0:N_CLASSES, 0].astype(jnp.int32)
    tiles = (cnt + TMS - 1) // TMS
    ends = jnp.cumsum(tiles)
    starts = (ends - tiles) * TMS
    n_active = ends[-1]
    pos_flat = pos.reshape(-1)
    pos_sorted = jnp.take(starts, pos_flat // cap) + pos_flat % cap
    fill_tile = jnp.where(tiles > 0, ends - 1, -1)
    slot = jnp.minimum(jnp.arange(n_slots, dtype=jnp.int32), n_active - 1)
    tile_cls = jnp.sum((slot[:, None] >= ends[None, :]).astype(jnp.int32), axis=1)
    pair_lo = jnp.array([0, 0, 0, 1, 1, 2], jnp.int32)
    pair_hi = jnp.array([1, 2, 3, 2, 3, 3], jnp.int32)
    first = (tile_cls // PAIRS_PER_GROUP) * EXPERTS_PER_GROUP
    tile_lo = first + jnp.take(pair_lo, tile_cls % PAIRS_PER_GROUP)
    tile_hi = first + jnp.take(pair_hi, tile_cls % PAIRS_PER_GROUP)
    i32 = lambda a: a.astype(jnp.int32)
    return i32(pos_sorted), i32(fill_tile), i32(tile_lo), i32(tile_hi), i32(n_active.reshape(1))


def _moe(x_mid, hrow, pos, counts, mod_l, wg, wu, wd, layer, final_g, dims, cap, final, interpret):
    rows = x_mid.shape[0]
    n_slots = rows // TMS + N_CLASSES
    pos_sorted, fill_tile, tile_lo, tile_hi, n_active = _sorted_layout(pos, counts, cap, n_slots)
    nb, s_len, c_len = dims
    td = TD if (s_len % TD == 0 and (nb * c_len) % TD == 0) else TM
    xs = _dispatch(hrow, pos_sorted, fill_tile, n_active, n_slots, td, interpret)
    ys = _experts(xs, tile_lo, tile_hi, n_active, wg, wu, wd, layer, interpret)
    return _combine(ys, pos_sorted, x_mid, mod_l, final_g, dims, td, final, interpret)


def _rope_tables(s_len):
    rows = s_len // GRID_W
    quarter = HEAD_DIM // 4
    r, col = jnp.meshgrid(jnp.arange(rows, dtype=F32), jnp.arange(GRID_W, dtype=F32), indexing="ij")
    inv_freq = ROPE_THETA ** (-jnp.arange(0, HEAD_DIM // 2, 2, dtype=F32) / (HEAD_DIM // 2))
    ang_r = r.reshape(-1)[:, None] * inv_freq
    ang_c = col.reshape(-1)[:, None] * inv_freq
    cos = jnp.concatenate([jnp.cos(ang_r)] * 2 + [jnp.cos(ang_c)] * 2, axis=-1)
    sin = jnp.concatenate([-jnp.sin(ang_r), jnp.sin(ang_r), -jnp.sin(ang_c), jnp.sin(ang_c)], axis=-1)
    assert cos.shape == (s_len, 4 * quarter)
    return cos, sin


def _block_diag_tiles(w):
    per = CT // LRU_HEAD_DIM
    n_tiles = D_LRU // CT
    w = w.reshape(2, n_tiles, per, LRU_HEAD_DIM, LRU_HEAD_DIM)
    eye = jnp.eye(per, dtype=w.dtype)
    full = jnp.einsum("dtpij,pq->dtpiqj", w, eye)
    return full.reshape(2, n_tiles, CT, CT).astype(BF16)


def _forward(x, c, ctx, c_ctx, ada_w, ada_b, norm_mix_g, norm_ffn_g, ev_w_in, ev_w_out,
             lru_conv_w, lru_conv_b, lru_w_r, lru_b_r, lru_w_i, lru_b_i, lru_lambda,
             cf_dw_w, cf_dw_b, cf_ln_g, cf_ln_b, at_w_qkv, at_w_out, at_q_norm_g, at_k_norm_g,
             router_w, router_bias, exp_w_gate, exp_w_up, exp_w_down, final_norm_g, interpret=False):
    nb, s_len, d = x.shape
    c_len = ctx.shape[1]
    depth = ada_w.shape[0]
    dims = (nb, s_len, c_len)
    assert d == D_MODEL and nb < SUBLANES
    assert s_len % TM == 0 and (nb * c_len) % TM == 0 and s_len % c_len == 0 and s_len % TQ == 0
    n_lat_tiles = nb * s_len // TM
    n_all_tiles = n_lat_tiles + nb * c_len // TM

    cvec = jnp.zeros((SUBLANES, d), F32).at[:nb].set(c).at[nb].set(c_ctx)
    mod = _adaln(cvec, ada_w, ada_b, interpret)
    xf = jnp.concatenate([x.reshape(nb * s_len, d), ctx.reshape(nb * c_len, d)], axis=0)

    rope_cos, rope_sin = _rope_tables(s_len)
    rw = jnp.zeros((d, LANES), F32).at[:, :N_EXPERTS].set(router_w)
    rw_hi, rw_lo = _split_bf16(rw)
    rb = router_bias.astype(F32)
    wg_all, wu_all, wd_all = exp_w_gate.astype(BF16), exp_w_up.astype(BF16), exp_w_down.astype(BF16)

    for l in range(depth):
        last = l == depth - 1
        n_tiles = n_lat_tiles if last else n_all_tiles
        mod_l = mod[l]
        if l % 2 == 0:
            e = l // 2
            u_all = _even_in(xf, mod_l, norm_mix_g[l], ev_w_in[e].astype(BF16), dims, interpret)
            p = {
                "conv_w": lru_conv_w[e], "conv_b": lru_conv_b[e].reshape(1, D_LRU),
                "w_r": _block_diag_tiles(lru_w_r[e]), "b_r": lru_b_r[e].reshape(2, 1, D_LRU),
                "w_i": _block_diag_tiles(lru_w_i[e]), "b_i": lru_b_i[e].reshape(2, 1, D_LRU),
                "lam": lru_lambda[e].reshape(2, 1, D_LRU),
                "dw_w": cf_dw_w[e], "dw_b": cf_dw_b[e].reshape(1, D_CONV),
            }
            y_lat, y_ctx = _seq_mix(u_all, p, dims, interpret)
            w_out = ev_w_out[e].astype(BF16)
            ln_g, ln_b = cf_ln_g[e], cf_ln_b[e]
        else:
            o = l // 2
            qkv = _odd_in(xf, mod_l, norm_mix_g[l], at_w_qkv[o].astype(BF16), at_q_norm_g[o], at_k_norm_g[o],
                          rope_cos, rope_sin, dims, interpret)
            y_lat = _attention(qkv, dims, True, interpret)
            y_ctx = y_lat if last else _attention(qkv, dims, False, interpret)
            w_out = at_w_out[o].astype(BF16)
            ln_g, ln_b = cf_ln_g[0], cf_ln_b[0]
        cap = n_tiles * TM + TMS
        x_mid, hrow, pos, counts = _mix_out(y_lat, y_ctx, xf, mod_l, w_out, ln_g, ln_b, norm_ffn_g[l], rw_hi, rw_lo,
                                            rb, dims, n_tiles, cap, l % 2 == 0, interpret)
        xf = _moe(x_mid, hrow, pos, counts, mod_l, wg_all, wu_all, wd_all, l, final_norm_g, dims, cap, last, interpret)
    return xf.reshape(nb, s_len, d)


def kernel(x, c, ctx, c_ctx, ada_w, ada_b, norm_mix_g, norm_ffn_g, ev_w_in, ev_w_out, lru_conv_w, lru_conv_b,
           lru_w_r, lru_b_r, lru_w_i, lru_b_i, lru_lambda, cf_dw_w, cf_dw_b, cf_ln_g, cf_ln_b, at_w_qkv, at_w_out,
           at_q_norm_g, at_k_norm_g, router_w, router_bias, exp_w_gate, exp_w_up, exp_w_down, final_norm_g):
    return _forward(x, c, ctx, c_ctx, ada_w, ada_b, norm_mix_g, norm_ffn_g, ev_w_in, ev_w_out, lru_conv_w,
                    lru_conv_b, lru_w_r, lru_b_r, lru_w_i, lru_b_i, lru_lambda, cf_dw_w, cf_dw_b, cf_ln_g,
                    cf_ln_b, at_w_qkv, at_w_out, at_q_norm_g, at_k_norm_g, router_w, router_bias, exp_w_gate,
                    exp_w_up, exp_w_down, final_norm_g)
```

```python
import functools
import math

import jax
import jax.numpy as jnp
from jax import lax
from jax.experimental import pallas as pl
from jax.experimental.pallas import tpu as pltpu

F32 = jnp.float32
BF16 = jnp.bfloat16

D_MODEL = 1024
N_MOD = 6
NORM_EPS = 1e-6
GRID_W = 64

D_LRU = 512
LRU_HEADS = 8
LRU_HEAD_DIM = D_LRU // LRU_HEADS
LRU_CONV_W = 4
LRU_C = 8.0
D_CONV = 512
CONF_WIDTH = 31
CONF_PAD = 16

HEAD_DIM = 128
N_Q_HEADS = 8
N_KV_HEADS = 2
GQA_GROUP = N_Q_HEADS // N_KV_HEADS
ROPE_THETA = 10000.0

N_EXPERTS = 16
N_GROUPS = 4
EXPERTS_PER_GROUP = 4
D_EXPERT = 512

LANES = 128
SUBLANES = 8
TM = 512
TMS = 256
PAIRS_PER_GROUP = 6
N_CLASSES = N_GROUPS * PAIRS_PER_GROUP
CLASS_ROWS = 32
ISSUE_UNROLL = 8
TD = 1024
ROW_W = D_MODEL + LANES
CT = 128
TQ = 512
TK = 1024
CONV_CHUNK = 128
LRU_CHUNK = 256
CARRY_UNROLL = 4
VMEM_LIMIT = 56 * 1024 * 1024


def _cparams(sem):
    return pltpu.CompilerParams(dimension_semantics=sem, vmem_limit_bytes=VMEM_LIMIT)


def _rms(x, g):
    return x * lax.rsqrt(jnp.mean(x * x, axis=-1, keepdims=True) + NORM_EPS) * g


def _split_bf16(a):
    hi = a.astype(BF16)
    lo = (a - hi.astype(F32)).astype(BF16)
    return hi, lo


def _dot(a, b):
    return jnp.dot(a, b, preferred_element_type=F32)


def _dot3(a_hi, a_lo, b_hi, b_lo):
    return _dot(a_hi, b_hi) + (_dot(a_lo, b_hi) + _dot(a_hi, b_lo))


def _mod_row(i, n_lat_tiles, tiles_per_batch, nb):
    return jnp.where(i < n_lat_tiles, i // tiles_per_batch, nb)


def _adaln_kernel(c_ref, w_ref, b_ref, o_ref):
    cv = c_ref[...]
    s = cv * jax.nn.sigmoid(cv)
    s_hi, s_lo = _split_bf16(s)
    w_hi, w_lo = _split_bf16(w_ref[...])
    o_ref[...] = _dot3(s_hi, s_lo, w_hi, w_lo) + b_ref[...]


def _adaln(cvec, ada_w, ada_b, interpret):
    depth = ada_w.shape[0]
    n_out = ada_w.shape[2]
    tn = D_MODEL
    return pl.pallas_call(
        _adaln_kernel,
        out_shape=jax.ShapeDtypeStruct((depth, SUBLANES, n_out), F32),
        grid=(depth, n_out // tn),
        in_specs=[
            pl.BlockSpec((SUBLANES, D_MODEL), lambda l, j: (0, 0)),
            pl.BlockSpec((None, D_MODEL, tn), lambda l, j: (l, 0, j)),
            pl.BlockSpec((None, 1, tn), lambda l, j: (l, 0, j)),
        ],
        out_specs=pl.BlockSpec((None, SUBLANES, tn), lambda l, j: (l, 0, j)),
        compiler_params=_cparams(("arbitrary", "arbitrary")),
        name="adaln_mod",
        interpret=interpret,
    )(cvec, ada_w, ada_b.reshape(depth, 1, n_out))


def _even_in_kernel(x_ref, mod_ref, g_ref, w_ref, o_ref, *, n_lat_tiles, tiles_per_batch, nb):
    i = pl.program_id(0)
    row = _mod_row(i, n_lat_tiles, tiles_per_batch, nb)
    sh = mod_ref[pl.ds(row, 1), 0:D_MODEL]
    sc = mod_ref[pl.ds(row, 1), D_MODEL:2 * D_MODEL]
    h = _rms(x_ref[...], g_ref[...]) * (1.0 + sc) + sh
    u = _dot(h.astype(BF16), w_ref[...])
    o_ref[:, 0:D_LRU] = u[:, 0:D_LRU]
    o_ref[:, D_LRU:2 * D_LRU] = jax.nn.gelu(u[:, D_LRU:2 * D_LRU])
    val = u[:, 2 * D_LRU:2 * D_LRU + D_CONV]
    gate = u[:, 2 * D_LRU + D_CONV:]
    o_ref[:, 2 * D_LRU:] = val * jax.nn.sigmoid(gate)


def _even_in(x, mod_l, g, w_in, dims, interpret):
    nb, s_len, c_len = dims
    t = x.shape[0]
    n_out = 2 * D_LRU + D_CONV
    kern = functools.partial(_even_in_kernel, n_lat_tiles=nb * s_len // TM,
                             tiles_per_batch=s_len // TM, nb=nb)
    return pl.pallas_call(
        kern,
        out_shape=jax.ShapeDtypeStruct((t, n_out), F32),
        grid=(t // TM,),
        in_specs=[
            pl.BlockSpec((TM, D_MODEL), lambda i: (i, 0)),
            pl.BlockSpec(mod_l.shape, lambda i: (0, 0)),
            pl.BlockSpec((1, D_MODEL), lambda i: (0, 0)),
            pl.BlockSpec(w_in.shape, lambda i: (0, 0)),
        ],
        out_specs=pl.BlockSpec((TM, n_out), lambda i: (i, 0)),
        compiler_params=_cparams(("arbitrary",)),
        name="even_in",
        interpret=interpret,
    )(x, mod_l, g.reshape(1, D_MODEL), w_in)


def _block_scan(a, b, reverse):
    n, width = a.shape
    a = a.reshape(n // SUBLANES, SUBLANES, width)
    b = b.reshape(n // SUBLANES, SUBLANES, width)
    tmod = lax.broadcasted_iota(jnp.int32, a.shape, 1)
    d = 1
    while d < SUBLANES:
        shift = SUBLANES - d if reverse else d
        keep = (tmod < SUBLANES - d) if reverse else (tmod >= d)
        a_n = pltpu.roll(a, shift, axis=1)
        b_n = pltpu.roll(b, shift, axis=1)
        b = a * jnp.where(keep, b_n, 0.0) + b
        a = a * jnp.where(keep, a_n, 1.0)
        d *= 2
    return a.reshape(n, width), b.reshape(n, width)


def _lru_coeffs(u, ub, wr, br, wi, bi, nsp):
    r = jax.nn.sigmoid(_dot(ub, wr) + br)
    i = jax.nn.sigmoid(_dot(ub, wi) + bi)
    log_a = (-LRU_C) * r * nsp
    a = jnp.exp(log_a)
    b = jnp.sqrt(1.0 - a * a) * (i * u)
    return a, b


def _seq_kernel(xl_ref, xc_ref, gl_ref, gc_ref, vl_ref, vc_ref,
                cw_ref, cb_ref, wr_ref, br_ref, wi_ref, bi_ref, sp_ref, dw_ref, db_ref,
                ol_ref, oc_ref,
                af_ref, bf_ref, ar_ref, br_sc_ref, hf_ref, hr_ref, xp_ref, ph_ref, *, n_lru_tiles):
    c = pl.program_id(1)
    s_len = xl_ref.shape[0]
    c_len = xc_ref.shape[0]

    @pl.when(c < n_lru_tiles)
    def _lru():
        cw = cw_ref[...]
        cb = cb_ref[...]
        nsp = [jax.nn.softplus(-sp_ref[d]) for d in range(2)]

        def coeffs(x_ref, n):
            zpad = jnp.zeros((SUBLANES, CT), F32)
            xp_ref[0:SUBLANES] = zpad
            xp_ref[SUBLANES:SUBLANES + n] = x_ref[...]
            xp_ref[SUBLANES + n:2 * SUBLANES + n] = zpad
            ch = min(LRU_CHUNK, n)

            def body(j, _):
                t0 = pl.multiple_of(j * ch, ch)
                win = xp_ref[pl.ds(t0, ch + 2 * SUBLANES)]
                off = SUBLANES - LRU_CONV_W // 2
                u = cb
                for k in range(LRU_CONV_W):
                    u = u + cw[k:k + 1] * win[off + k:off + k + ch]
                ub = u.astype(BF16)
                a_f, b_f = _lru_coeffs(u, ub, wr_ref[0], br_ref[0], wi_ref[0], bi_ref[0], nsp[0])
                a_f, b_f = _block_scan(a_f, b_f, False)
                af_ref[pl.ds(t0, ch)] = a_f
                bf_ref[pl.ds(t0, ch)] = b_f
                a_r, b_r = _lru_coeffs(u, ub, wr_ref[1], br_ref[1], wi_ref[1], bi_ref[1], nsp[1])
                a_r, b_r = _block_scan(a_r, b_r, True)
                ar_ref[pl.ds(t0, ch)] = a_r
                br_sc_ref[pl.ds(t0, ch)] = b_r
                return 0

            lax.fori_loop(0, n // ch, body, 0)

        def carry_pass(n, h0f, h0r):
            nblk = n // SUBLANES

            def body(j, carry):
                hf, hr = carry
                for u in range(CARRY_UNROLL):
                    k = j * CARRY_UNROLL + u
                    kf = pl.multiple_of(k * SUBLANES, SUBLANES)
                    kr = pl.multiple_of((nblk - 1 - k) * SUBLANES, SUBLANES)
                    yf = bf_ref[pl.ds(kf, SUBLANES)] + af_ref[pl.ds(kf, SUBLANES)] * hf
                    yr = br_sc_ref[pl.ds(kr, SUBLANES)] + ar_ref[pl.ds(kr, SUBLANES)] * hr
                    hf_ref[pl.ds(kf, SUBLANES)] = yf
                    hr_ref[pl.ds(kr, SUBLANES)] = yr
                    hf, hr = yf[SUBLANES - 1:SUBLANES], yr[0:1]
                return hf, hr

            return lax.fori_loop(0, nblk // CARRY_UNROLL, body, (h0f, h0r))

        zero = jnp.zeros((1, CT), F32)
        coeffs(xc_ref, c_len)
        hcf, hcr = carry_pass(c_len, zero, zero)
        oc_ref[...] = (hf_ref[0:c_len] + hr_ref[0:c_len]) * gc_ref[...]
        coeffs(xl_ref, s_len)
        carry_pass(s_len, hcf, hcr)
        ol_ref[...] = (hf_ref[0:s_len] + hr_ref[0:s_len]) * gl_ref[...]

    @pl.when(c >= n_lru_tiles)
    def _conv():
        dw = dw_ref[...]
        db = db_ref[...]
        zpad = jnp.zeros((CONF_PAD, CT), F32)

        def conv(v_ref, o_ref, n):
            xp_ref[0:CONF_PAD] = zpad
            xp_ref[CONF_PAD:CONF_PAD + n] = v_ref[...]
            xp_ref[CONF_PAD + n:2 * CONF_PAD + n] = zpad

            def body(j, _):
                t0 = pl.multiple_of(j * CONV_CHUNK, CONV_CHUNK)
                win = xp_ref[pl.ds(t0, CONV_CHUNK + 2 * CONF_PAD)]
                acc = jnp.broadcast_to(db, (CONV_CHUNK, CT))
                off = CONF_PAD - CONF_WIDTH // 2
                span = CONV_CHUNK + 2 * CONF_PAD - SUBLANES
                for r in range(SUBLANES):
                    ph_ref[r] = win[r:r + span]
                for k in range(CONF_WIDTH):
                    r = (off + k) % SUBLANES
                    q = (off + k) // SUBLANES * SUBLANES
                    acc = acc + dw[k:k + 1] * ph_ref[r, q:q + CONV_CHUNK]
                o_ref[pl.ds(t0, CONV_CHUNK)] = acc
                return 0

            lax.fori_loop(0, n // CONV_CHUNK, body, 0)

        conv(vc_ref, oc_ref, c_len)
        conv(vl_ref, ol_ref, s_len)


def _seq_mix(u_all, p, dims, interpret):
    nb, s_len, c_len = dims
    n_lru = D_LRU // CT
    n_cv = D_CONV // CT
    lat_blocks = nb * s_len // c_len

    def lru_c(c):
        return jnp.minimum(c, n_lru - 1)

    def cv_c(c):
        return jnp.maximum(c - n_lru, 0)

    in_specs = [
        pl.BlockSpec((s_len, CT), lambda b, c: (b, lru_c(c))),
        pl.BlockSpec((c_len, CT), lambda b, c: (lat_blocks + b, lru_c(c))),
        pl.BlockSpec((s_len, CT), lambda b, c: (b, n_lru + lru_c(c))),
        pl.BlockSpec((c_len, CT), lambda b, c: (lat_blocks + b, n_lru + lru_c(c))),
        pl.BlockSpec((s_len, CT), lambda b, c: (b, 2 * n_lru + cv_c(c))),
        pl.BlockSpec((c_len, CT), lambda b, c: (lat_blocks + b, 2 * n_lru + cv_c(c))),
        pl.BlockSpec((LRU_CONV_W, CT), lambda b, c: (0, lru_c(c))),
        pl.BlockSpec((1, CT), lambda b, c: (0, lru_c(c))),
        pl.BlockSpec((2, None, CT, CT), lambda b, c: (0, lru_c(c), 0, 0)),
        pl.BlockSpec((2, 1, CT), lambda b, c: (0, 0, lru_c(c))),
        pl.BlockSpec((2, None, CT, CT), lambda b, c: (0, lru_c(c), 0, 0)),
        pl.BlockSpec((2, 1, CT), lambda b, c: (0, 0, lru_c(c))),
        pl.BlockSpec((2, 1, CT), lambda b, c: (0, 0, lru_c(c))),
        pl.BlockSpec((CONF_WIDTH, CT), lambda b, c: (0, cv_c(c))),
        pl.BlockSpec((1, CT), lambda b, c: (0, cv_c(c))),
    ]
    out_specs = [
        pl.BlockSpec((s_len, CT), lambda b, c: (b, c)),
        pl.BlockSpec((c_len, CT), lambda b, c: (b, c)),
    ]
    scratch = [pltpu.VMEM((s_len, CT), F32) for _ in range(6)]
    scratch.append(pltpu.VMEM((s_len + 2 * CONF_PAD, CT), F32))
    scratch.append(pltpu.VMEM((SUBLANES, CONV_CHUNK + 2 * CONF_PAD - SUBLANES, CT), F32))
    return pl.pallas_call(
        functools.partial(_seq_kernel, n_lru_tiles=n_lru),
        out_shape=[jax.ShapeDtypeStruct((nb * s_len, D_LRU + D_CONV), F32),
                   jax.ShapeDtypeStruct((nb * c_len, D_LRU + D_CONV), F32)],
        grid=(nb, n_lru + n_cv),
        in_specs=in_specs,
        out_specs=out_specs,
        scratch_shapes=scratch,
        compiler_params=_cparams(("arbitrary", "arbitrary")),
        name="seq_mix",
        interpret=interpret,
    )(u_all, u_all, u_all, u_all, u_all, u_all,
      p["conv_w"], p["conv_b"], p["w_r"], p["b_r"], p["w_i"], p["b_i"], p["lam"], p["dw_w"], p["dw_b"])


def _swap32(x):
    lane = lax.broadcasted_iota(jnp.int32, x.shape, 1)
    return jnp.where((lane & 32) == 0, pltpu.roll(x, HEAD_DIM - 32, axis=1), pltpu.roll(x, 32, axis=1))


def _odd_in_kernel(x_ref, mod_ref, g_ref, w_ref, qg_ref, kg_ref, cos_ref, sin_ref, o_ref,
                   *, n_lat_tiles, tiles_per_batch, nb):
    i = pl.program_id(0)
    row = _mod_row(i, n_lat_tiles, tiles_per_batch, nb)
    sh = mod_ref[pl.ds(row, 1), 0:D_MODEL]
    sc = mod_ref[pl.ds(row, 1), D_MODEL:2 * D_MODEL]
    h = _rms(x_ref[...], g_ref[...]) * (1.0 + sc) + sh
    u = _dot(h.astype(BF16), w_ref[...])
    is_lat = i < n_lat_tiles
    cos = jnp.where(is_lat, cos_ref[...], 1.0)
    sin = jnp.where(is_lat, sin_ref[...], 0.0)
    q_scale = math.log2(math.e) / math.sqrt(HEAD_DIM)
    for hh in range(N_Q_HEADS + N_KV_HEADS):
        xh = u[:, hh * HEAD_DIM:(hh + 1) * HEAD_DIM]
        gain = qg_ref[...] if hh < N_Q_HEADS else kg_ref[...]
        xn = _rms(xh, gain)
        y = xn * cos + _swap32(xn) * sin
        if hh < N_Q_HEADS:
            y = y * q_scale
        o_ref[:, hh * HEAD_DIM:(hh + 1) * HEAD_DIM] = y.astype(BF16)
    v0 = (N_Q_HEADS + N_KV_HEADS) * HEAD_DIM
    o_ref[:, v0:] = u[:, v0:].astype(BF16)


def _odd_in(x, mod_l, g, w_qkv, q_g, k_g, rope_cos, rope_sin, dims, interpret):
    nb, s_len, c_len = dims
    t = x.shape[0]
    n_out = w_qkv.shape[1]
    tpb = s_len // TM
    n_lat = nb * tpb
    kern = functools.partial(_odd_in_kernel, n_lat_tiles=n_lat, tiles_per_batch=tpb, nb=nb)

    def pos_block(i):
        return jnp.where(i < n_lat, i % tpb, 0)

    return pl.pallas_call(
        kern,
        out_shape=jax.ShapeDtypeStruct((t, n_out), BF16),
        grid=(t // TM,),
        in_specs=[
            pl.BlockSpec((TM, D_MODEL), lambda i: (i, 0)),
            pl.BlockSpec(mod_l.shape, lambda i: (0, 0)),
            pl.BlockSpec((1, D_MODEL), lambda i: (0, 0)),
            pl.BlockSpec(w_qkv.shape, lambda i: (0, 0)),
            pl.BlockSpec((1, HEAD_DIM), lambda i: (0, 0)),
            pl.BlockSpec((1, HEAD_DIM), lambda i: (0, 0)),
            pl.BlockSpec((TM, HEAD_DIM), lambda i: (pos_block(i), 0)),
            pl.BlockSpec((TM, HEAD_DIM), lambda i: (pos_block(i), 0)),
        ],
        out_specs=pl.BlockSpec((TM, n_out), lambda i: (i, 0)),
        compiler_params=_cparams(("arbitrary",)),
        name="odd_in",
        interpret=interpret,
    )(x, mod_l, g.reshape(1, D_MODEL), w_qkv, q_g.reshape(1, HEAD_DIM), k_g.reshape(1, HEAD_DIM),
      rope_cos, rope_sin)


def _attn_kernel(*refs, n_seg):
    q_ref = refs[0]
    k_refs = refs[1:1 + n_seg]
    v_refs = refs[1 + n_seg:1 + 2 * n_seg]
    o_ref = refs[1 + 2 * n_seg]
    tq = q_ref.shape[0]
    chunks = []
    for k_ref, v_ref in zip(k_refs, v_refs):
        n = k_ref.shape[0]
        step = min(TK, n)
        for s0 in range(0, n, step):
            chunks.append((k_ref, v_ref, s0, step))
    for g in range(GQA_GROUP):
        q = q_ref[:, g * HEAD_DIM:(g + 1) * HEAD_DIM]
        m = jnp.full((tq, 1), -jnp.inf, F32)
        acc = jnp.zeros((tq, 2 * HEAD_DIM), F32)
        for k_ref, v_ref, s0, step in chunks:
            k = k_ref[s0:s0 + step, :]
            v1 = jnp.concatenate([v_ref[s0:s0 + step, :], jnp.ones((step, HEAD_DIM), BF16)], axis=1)
            s = lax.dot_general(q, k, (((1,), (1,)), ((), ())), preferred_element_type=F32)
            m_new = jnp.maximum(m, jnp.max(s, axis=-1, keepdims=True))
            p = jnp.exp2(s - m_new)
            acc = jnp.exp2(m - m_new) * acc + _dot(p.astype(BF16), v1)
            m = m_new
        o_ref[:, g * HEAD_DIM:(g + 1) * HEAD_DIM] = (acc[:, 0:HEAD_DIM] / acc[:, HEAD_DIM:HEAD_DIM + 1]).astype(BF16)


def _attention(qkv, dims, latent, interpret):
    nb, s_len, c_len = dims
    k_col = N_Q_HEADS
    v_col = N_Q_HEADS + N_KV_HEADS
    lat_blocks = nb * s_len // c_len
    ctx_k = pl.BlockSpec((c_len, HEAD_DIM), lambda b, h, qi: (lat_blocks + b, k_col + h))
    ctx_v = pl.BlockSpec((c_len, HEAD_DIM), lambda b, h, qi: (lat_blocks + b, v_col + h))
    if latent:
        tq = TQ
        nq = s_len // tq
        q_spec = pl.BlockSpec((tq, GQA_GROUP * HEAD_DIM), lambda b, h, qi: (b * nq + qi, h))
        k_specs = [pl.BlockSpec((s_len, HEAD_DIM), lambda b, h, qi: (b, k_col + h)), ctx_k]
        v_specs = [pl.BlockSpec((s_len, HEAD_DIM), lambda b, h, qi: (b, v_col + h)), ctx_v]
        out_rows = nb * s_len
        o_spec = pl.BlockSpec((tq, GQA_GROUP * HEAD_DIM), lambda b, h, qi: (b * nq + qi, h))
    else:
        tq = c_len
        nq = 1
        q_spec = pl.BlockSpec((tq, GQA_GROUP * HEAD_DIM), lambda b, h, qi: (lat_blocks + b, h))
        k_specs = [ctx_k]
        v_specs = [ctx_v]
        out_rows = nb * c_len
        o_spec = pl.BlockSpec((tq, GQA_GROUP * HEAD_DIM), lambda b, h, qi: (b, h))
    n_seg = len(k_specs)
    return pl.pallas_call(
        functools.partial(_attn_kernel, n_seg=n_seg),
        out_shape=jax.ShapeDtypeStruct((out_rows, N_Q_HEADS * HEAD_DIM), BF16),
        grid=(nb, N_KV_HEADS, nq),
        in_specs=[q_spec] + k_specs + v_specs,
        out_specs=o_spec,
        compiler_params=_cparams(("arbitrary", "arbitrary", "arbitrary")),
        name="attn_lat" if latent else "attn_ctx",
        interpret=interpret,
    )(*([qkv] * (1 + 2 * n_seg)))


def _route(logits_t, bias):
    score = [jax.nn.sigmoid(logits_t[e:e + 1]) for e in range(N_EXPERTS)]
    sel = [score[e] + bias[e] for e in range(N_EXPERTS)]
    grp = []
    for g in range(N_GROUPS):
        a, b, c, d = sel[4 * g:4 * g + 4]
        hi1, lo1 = jnp.maximum(a, b), jnp.minimum(a, b)
        hi2, lo2 = jnp.maximum(c, d), jnp.minimum(c, d)
        top1 = jnp.maximum(hi1, hi2)
        top2 = jnp.maximum(jnp.minimum(hi1, hi2), jnp.maximum(lo1, lo2))
        grp.append(top1 + top2)
    best = jnp.zeros_like(grp[0], dtype=jnp.int32)
    best_s = grp[0]
    for g in range(1, N_GROUPS):
        better = grp[g] > best_s
        best = jnp.where(better, g, best)
        best_s = jnp.where(better, grp[g], best_s)
    chosen = []
    for e in range(N_EXPERTS):
        g = e // EXPERTS_PER_GROUP
        beaten = jnp.zeros_like(best)
        for k in range(4 * g, 4 * g + 4):
            if k == e:
                continue
            beats = (sel[k] >= sel[e]) if k < e else (sel[k] > sel[e])
            beaten = beaten + jnp.where(beats, 1, 0)
        chosen.append((beaten < 2) & (best == g))
    wsum = jnp.zeros_like(score[0])
    for e in range(N_EXPERTS):
        wsum = wsum + jnp.where(chosen[e], score[e], 0.0)
    picked, gate = [], []
    for j in range(EXPERTS_PER_GROUP):
        cj = jnp.zeros_like(best)
        gj = jnp.zeros_like(wsum)
        for g in range(N_GROUPS):
            e = EXPERTS_PER_GROUP * g + j
            cj = jnp.where(chosen[e], 1, cj)
            gj = jnp.where(best == g, score[e], gj)
        picked.append(cj > 0)
        gate.append(gj / wsum)
    j_lo = jnp.where(picked[0], 0, jnp.where(picked[1], 1, 2))
    j_hi = jnp.where(picked[3], 3, jnp.where(picked[2], 2, 1))
    pair = jnp.where(j_lo == 0, j_hi - 1, jnp.where(j_lo == 1, j_hi + 1, PAIRS_PER_GROUP - 1))
    gate_lo = jnp.where(j_lo == 0, gate[0], jnp.where(j_lo == 1, gate[1], gate[2]))
    gate_hi = jnp.where(j_hi == 3, gate[3], jnp.where(j_hi == 2, gate[2], gate[1]))
    return best * PAIRS_PER_GROUP + pair, gate_lo, gate_hi


def _mix_out_kernel(yl_ref, yc_ref, x_ref, mod_ref, w_ref, lng_ref, lnb_ref, nfg_ref, rwh_ref, rwl_ref, rb_ref,
                    xo_ref, hrow_ref, pos_ref, cnt_ref, base_ref,
                    *, even, n_lat_tiles, tiles_per_batch, nb, cap):
    i = pl.program_id(0)

    @pl.when(i == 0)
    def _init():
        base_ref[...] = jnp.zeros_like(base_ref)

    row = _mod_row(i, n_lat_tiles, tiles_per_batch, nb)
    g1 = mod_ref[pl.ds(row, 1), 2 * D_MODEL:3 * D_MODEL]
    sh2 = mod_ref[pl.ds(row, 1), 3 * D_MODEL:4 * D_MODEL]
    sc2 = mod_ref[pl.ds(row, 1), 4 * D_MODEL:5 * D_MODEL]
    is_lat = i < n_lat_tiles
    y = jnp.where(is_lat, yl_ref[...], yc_ref[...])
    if even:
        ya = y[:, 0:D_LRU]
        vc = y[:, D_LRU:]
        mu = jnp.mean(vc, axis=-1, keepdims=True)
        xc = vc - mu
        var = jnp.mean(xc * xc, axis=-1, keepdims=True)
        ln = xc * lax.rsqrt(var + NORM_EPS) * lng_ref[...] + lnb_ref[...]
        yb = ln * jax.nn.sigmoid(ln)
        out = _dot(ya.astype(BF16), w_ref[0:D_LRU, :]) + _dot(yb.astype(BF16), w_ref[D_LRU:, :])
    else:
        out = _dot(y, w_ref[...])
    x_new = x_ref[...] + g1 * out
    xo_ref[...] = x_new
    h2 = _rms(x_new, nfg_ref[...]) * (1.0 + sc2) + sh2
    hrow_ref[:, 0:D_MODEL] = h2
    h_hi, h_lo = _split_bf16(h2)
    logits = _dot3(h_hi, h_lo, rwh_ref[...], rwl_ref[...])
    logits_t = jnp.transpose(logits)[0:N_EXPERTS]
    cls, gate_lo, gate_hi = _route(logits_t, [rb_ref[e] for e in range(N_EXPERTS)])
    tm = logits.shape[0]

    sub = lax.broadcasted_iota(jnp.int32, (CLASS_ROWS, tm), 0)
    onehot = jnp.where(sub == cls, 1.0, 0.0)
    earlier = (lax.broadcasted_iota(jnp.int32, (tm, tm), 0) < lax.broadcasted_iota(jnp.int32, (tm, tm), 1))
    before = _dot(onehot.astype(BF16), jnp.where(earlier, 1.0, 0.0).astype(BF16))
    rank = jnp.sum(onehot * (before + base_ref[:, 0:1]), axis=0, keepdims=True)
    pos_ref[...] = cls * cap + rank.astype(jnp.int32)
    base_ref[...] = base_ref[...] + jnp.sum(onehot, axis=1, keepdims=True)
    cnt_ref[...] = base_ref[...]

    sub8 = lax.broadcasted_iota(jnp.int32, (SUBLANES, tm), 0)
    extras_t = jnp.where(sub8 == 0, gate_lo, jnp.where(sub8 == 1, gate_hi, 0.0))
    extras_t = jnp.concatenate([extras_t, jnp.zeros((LANES - SUBLANES, tm), F32)], axis=0)
    hrow_ref[:, D_MODEL:] = jnp.transpose(extras_t)


def _mix_out(y_lat, y_ctx, x, mod_l, w_out, ln_g, ln_b, nf_g, rw_hi, rw_lo, router_bias, dims, n_tiles, cap, even,
             interpret):
    nb, s_len, c_len = dims
    tpb = s_len // TM
    n_lat = nb * tpb
    n_ctx = max(nb * c_len // TM, 1)
    rows = n_tiles * TM
    kern = functools.partial(_mix_out_kernel, even=even, n_lat_tiles=n_lat, tiles_per_batch=tpb, nb=nb, cap=cap)
    vec = lambda n: pl.BlockSpec((1, n), lambda i: (0, 0))
    return pl.pallas_call(
        kern,
        out_shape=[jax.ShapeDtypeStruct((rows, D_MODEL), F32),
                   jax.ShapeDtypeStruct((rows, ROW_W), F32),
                   jax.ShapeDtypeStruct((n_tiles, 1, TM), jnp.int32),
                   jax.ShapeDtypeStruct((CLASS_ROWS, LANES), F32)],
        grid=(n_tiles,),
        in_specs=[
            pl.BlockSpec((TM, D_MODEL), lambda i: (jnp.minimum(i, n_lat - 1), 0)),
            pl.BlockSpec((TM, D_MODEL), lambda i: (jnp.clip(i - n_lat, 0, n_ctx - 1), 0)),
            pl.BlockSpec((TM, D_MODEL), lambda i: (i, 0)),
            pl.BlockSpec(mod_l.shape, lambda i: (0, 0)),
            pl.BlockSpec(w_out.shape, lambda i: (0, 0)),
            vec(D_CONV), vec(D_CONV), vec(D_MODEL),
            pl.BlockSpec(rw_hi.shape, lambda i: (0, 0)),
            pl.BlockSpec(rw_lo.shape, lambda i: (0, 0)),
            pl.BlockSpec(memory_space=pltpu.SMEM),
        ],
        out_specs=[
            pl.BlockSpec((TM, D_MODEL), lambda i: (i, 0)),
            pl.BlockSpec((TM, ROW_W), lambda i: (i, 0)),
            pl.BlockSpec((None, 1, TM), lambda i: (i, 0, 0)),
            pl.BlockSpec((CLASS_ROWS, LANES), lambda i: (0, 0)),
        ],
        scratch_shapes=[pltpu.VMEM((CLASS_ROWS, LANES), F32)],
        compiler_params=_cparams(("arbitrary",)),
        name="mix_out_even" if even else "mix_out_odd",
        interpret=interpret,
    )(y_lat, y_ctx, x, mod_l, w_out, ln_g.reshape(1, D_CONV), ln_b.reshape(1, D_CONV),
      nf_g.reshape(1, D_MODEL), rw_hi, rw_lo, router_bias)


def _row_copy(src_ref, src_row, dst_ref, dst_row, sem):
    return pltpu.make_async_copy(src_ref.at[pl.ds(src_row, 1)], dst_ref.at[pl.ds(dst_row, 1)], sem)


def _dispatch_kernel(pos_ref, fill_ref, nact_ref, h_ref, xs_ref, zero_ref, row_sem, fill_sem, tail_sem, *, n_slots):
    i = pl.program_id(0)
    tm = h_ref.shape[0]

    @pl.when(i == 0)
    def _fill():
        zero_ref[...] = jnp.zeros_like(zero_ref)
        def fill(tile, sem):
            return pltpu.make_async_copy(zero_ref, xs_ref.at[pl.ds(pl.multiple_of(tile * TMS, TMS), TMS)], sem)

        for phase in ("start", "wait"):
            for c in range(N_CLASSES):
                @pl.when(fill_ref[c] >= 0)
                def _last_tile():
                    getattr(fill(fill_ref[c], fill_sem.at[c]), phase)()

                @pl.when(n_slots - 1 - c >= nact_ref[0])
                def _unused_tile():
                    getattr(fill(n_slots - 1 - c, tail_sem.at[c]), phase)()

    def issue(j, _):
        for k in range(ISSUE_UNROLL):
            r = j * ISSUE_UNROLL + k
            _row_copy(h_ref, r, xs_ref, pos_ref[i * tm + r], row_sem).start()
        return 0

    lax.fori_loop(0, tm // ISSUE_UNROLL, issue, 0)
    pltpu.make_async_copy(h_ref, xs_ref.at[pl.ds(0, tm)], row_sem).wait()


def _dispatch(hrow, pos_flat, fill_tile, n_active, n_slots, td, interpret):
    rows = hrow.shape[0]
    return pl.pallas_call(
        functools.partial(_dispatch_kernel, n_slots=n_slots),
        out_shape=jax.ShapeDtypeStruct((n_slots * TMS, ROW_W), F32),
        grid_spec=pltpu.PrefetchScalarGridSpec(
            num_scalar_prefetch=3,
            grid=(rows // td,),
            in_specs=[pl.BlockSpec((td, ROW_W), lambda i, pos, fill, nact: (i, 0))],
            out_specs=pl.BlockSpec(memory_space=pl.ANY),
            scratch_shapes=[pltpu.VMEM((TMS, ROW_W), F32), pltpu.SemaphoreType.DMA,
                            pltpu.SemaphoreType.DMA((N_CLASSES,)), pltpu.SemaphoreType.DMA((N_CLASSES,))],
        ),
        compiler_params=_cparams(("arbitrary",)),
        name="moe_dispatch",
        interpret=interpret,
    )(pos_flat, fill_tile, n_active, hrow)


def _experts_kernel(elo_ref, ehi_ref, nact_ref, xs_ref, wg0_ref, wu0_ref, wd0_ref, wg1_ref, wu1_ref, wd1_ref,
                    ys_ref):
    i = pl.program_id(0)

    @pl.when(i < nact_ref[0])
    def _active():
        h = xs_ref[:, 0:D_MODEL].astype(BF16)
        out = None
        for lane, (wg_ref, wu_ref, wd_ref) in enumerate(((wg0_ref, wu0_ref, wd0_ref),
                                                         (wg1_ref, wu1_ref, wd1_ref))):
            a = _dot(h, wg_ref[...])
            u = _dot(h, wu_ref[...])
            hid = (a * jax.nn.sigmoid(a)) * u * xs_ref[:, D_MODEL + lane:D_MODEL + lane + 1]
            y = _dot(hid.astype(BF16), wd_ref[...])
            out = y if out is None else out + y
        ys_ref[...] = out

    @pl.when(i >= nact_ref[0])
    def _unused():
        ys_ref[...] = jnp.zeros_like(ys_ref)


def _experts(xs, tile_lo, tile_hi, n_active, wg, wu, wd, layer, interpret):
    n_slots = tile_lo.shape[0]
    lo_idx = lambda i, lo, hi, nact: (layer, lo[i], 0, 0)
    hi_idx = lambda i, lo, hi, nact: (layer, hi[i], 0, 0)
    w_in = (None, None, D_MODEL, D_EXPERT)
    w_out = (None, None, D_EXPERT, D_MODEL)
    return pl.pallas_call(
        _experts_kernel,
        out_shape=jax.ShapeDtypeStruct((n_slots * TMS, D_MODEL), F32),
        grid_spec=pltpu.PrefetchScalarGridSpec(
            num_scalar_prefetch=3,
            grid=(n_slots,),
            in_specs=[
                pl.BlockSpec((TMS, ROW_W), lambda i, lo, hi, nact: (jnp.minimum(i, nact[0] - 1), 0)),
                pl.BlockSpec(w_in, lo_idx), pl.BlockSpec(w_in, lo_idx), pl.BlockSpec(w_out, lo_idx),
                pl.BlockSpec(w_in, hi_idx), pl.BlockSpec(w_in, hi_idx), pl.BlockSpec(w_out, hi_idx),
            ],
            out_specs=pl.BlockSpec((TMS, D_MODEL), lambda i, lo, hi, nact: (i, 0)),
        ),
        compiler_params=_cparams(("arbitrary",)),
        name="moe_experts",
        interpret=interpret,
    )(tile_lo, tile_hi, n_active, xs, wg, wu, wd, wg, wu, wd)


def _combine_kernel(pos_ref, ys_ref, x_ref, mod_ref, fg_ref, o_ref, ybuf_ref, sem,
                    *, final, n_lat_tiles, tiles_per_batch, nb):
    i = pl.program_id(0)
    tm = x_ref.shape[0]

    def issue(j, _):
        for k in range(ISSUE_UNROLL):
            r = j * ISSUE_UNROLL + k
            _row_copy(ys_ref, pos_ref[i * tm + r], ybuf_ref, r, sem).start()
        return 0

    lax.fori_loop(0, tm // ISSUE_UNROLL, issue, 0)
    pltpu.make_async_copy(ys_ref.at[pl.ds(0, tm)], ybuf_ref, sem).wait()
    row = _mod_row(i, n_lat_tiles, tiles_per_batch, nb)
    g2 = mod_ref[pl.ds(row, 1), 5 * D_MODEL:6 * D_MODEL]
    x_new = x_ref[...] + g2 * ybuf_ref[...]
    if final:
        x_new = _rms(x_new, fg_ref[...])
    o_ref[...] = x_new


def _combine(ys, pos_flat, x, mod_l, final_g, dims, td, final, interpret):
    nb, s_len, c_len = dims
    rows = x.shape[0]
    tpb = s_len // td
    kern = functools.partial(_combine_kernel, final=final, n_lat_tiles=nb * tpb, tiles_per_batch=tpb, nb=nb)
    return pl.pallas_call(
        kern,
        out_shape=jax.ShapeDtypeStruct((rows, D_MODEL), F32),
        grid_spec=pltpu.PrefetchScalarGridSpec(
            num_scalar_prefetch=1,
            grid=(rows // td,),
            in_specs=[
                pl.BlockSpec(memory_space=pl.ANY),
                pl.BlockSpec((td, D_MODEL), lambda i, pos: (i, 0)),
                pl.BlockSpec(mod_l.shape, lambda i, pos: (0, 0)),
                pl.BlockSpec((1, D_MODEL), lambda i, pos: (0, 0)),
            ],
            out_specs=pl.BlockSpec((td, D_MODEL), lambda i, pos: (i, 0)),
            scratch_shapes=[pltpu.VMEM((td, D_MODEL), F32), pltpu.SemaphoreType.DMA],
        ),
        compiler_params=_cparams(("arbitrary",)),
        name="moe_combine_final" if final else "moe_combine",
        interpret=interpret,
    )(pos_flat, ys, x, mod_l, final_g.reshape(1, D_MODEL))


def _sorted_layout(pos, counts, cap, n_slots):
    cnt = counts[0:N_CLASSES, 0].astype(jnp.int32)
    tiles = (cnt + TMS - 1) // TMS
    ends = jnp.cumsum(tiles)
    starts = (ends - tiles) * TMS
    n_active = ends[-1]
    pos_flat = pos.reshape(-1)
    cls = pos_flat // cap
    classes = jnp.arange(N_CLASSES, dtype=jnp.int32)
    start_of = jnp.sum(jnp.where(cls[:, None] == classes[None, :], starts[None, :], 0), axis=1)
    pos_sorted = start_of + pos_flat % cap
    fill_tile = jnp.where(tiles > 0, ends - 1, -1)
    slot = jnp.minimum(jnp.arange(n_slots, dtype=jnp.int32), n_active - 1)
    tile_cls = jnp.sum((slot[:, None] >= ends[None, :]).astype(jnp.int32), axis=1)
    pair_lo = jnp.array([0, 0, 0, 1, 1, 2], jnp.int32)
    pair_hi = jnp.array([1, 2, 3, 2, 3, 3], jnp.int32)
    first = (tile_cls // PAIRS_PER_GROUP) * EXPERTS_PER_GROUP
    tile_lo = first + jnp.take(pair_lo, tile_cls % PAIRS_PER_GROUP)
    tile_hi = first + jnp.take(pair_hi, tile_cls % PAIRS_PER_GROUP)
    i32 = lambda a: a.astype(jnp.int32)
    return i32(pos_sorted), i32(fill_tile), i32(tile_lo), i32(tile_hi), i32(n_active.reshape(1))


def _moe(x_mid, hrow, pos, counts, mod_l, wg, wu, wd, layer, final_g, dims, cap, final, interpret):
    rows = x_mid.shape[0]
    n_slots = rows // TMS + N_CLASSES
    pos_sorted, fill_tile, tile_lo, tile_hi, n_active = _sorted_layout(pos, counts, cap, n_slots)
    nb, s_len, c_len = dims
    td = TD if (s_len % TD == 0 and (nb * c_len) % TD == 0) else TM
    xs = _dispatch(hrow, pos_sorted, fill_tile, n_active, n_slots, td, interpret)
    ys = _experts(xs, tile_lo, tile_hi, n_active, wg, wu, wd, layer, interpret)
    return _combine(ys, pos_sorted, x_mid, mod_l, final_g, dims, td, final, interpret)


def _rope_tables(s_len):
    rows = s_len // GRID_W
    quarter = HEAD_DIM // 4
    r, col = jnp.meshgrid(jnp.arange(rows, dtype=F32), jnp.arange(GRID_W, dtype=F32), indexing="ij")
    inv_freq = ROPE_THETA ** (-jnp.arange(0, HEAD_DIM // 2, 2, dtype=F32) / (HEAD_DIM // 2))
    ang_r = r.reshape(-1)[:, None] * inv_freq
    ang_c = col.reshape(-1)[:, None] * inv_freq
    cos = jnp.concatenate([jnp.cos(ang_r)] * 2 + [jnp.cos(ang_c)] * 2, axis=-1)
    sin = jnp.concatenate([-jnp.sin(ang_r), jnp.sin(ang_r), -jnp.sin(ang_c), jnp.sin(ang_c)], axis=-1)
    assert cos.shape == (s_len, 4 * quarter)
    return cos, sin


def _block_diag_tiles(w):
    per = CT // LRU_HEAD_DIM
    n_tiles = D_LRU // CT
    w = w.reshape(2, n_tiles, per, LRU_HEAD_DIM, LRU_HEAD_DIM)
    eye = jnp.eye(per, dtype=w.dtype)
    full = jnp.einsum("dtpij,pq->dtpiqj", w, eye)
    return full.reshape(2, n_tiles, CT, CT).astype(BF16)


def _forward(x, c, ctx, c_ctx, ada_w, ada_b, norm_mix_g, norm_ffn_g, ev_w_in, ev_w_out,
             lru_conv_w, lru_conv_b, lru_w_r, lru_b_r, lru_w_i, lru_b_i, lru_lambda,
             cf_dw_w, cf_dw_b, cf_ln_g, cf_ln_b, at_w_qkv, at_w_out, at_q_norm_g, at_k_norm_g,
             router_w, router_bias, exp_w_gate, exp_w_up, exp_w_down, final_norm_g, interpret=False):
    nb, s_len, d = x.shape
    c_len = ctx.shape[1]
    depth = ada_w.shape[0]
    dims = (nb, s_len, c_len)
    assert d == D_MODEL and nb < SUBLANES
    assert s_len % TM == 0 and (nb * c_len) % TM == 0 and s_len % c_len == 0 and s_len % TQ == 0
    n_lat_tiles = nb * s_len // TM
    n_all_tiles = n_lat_tiles + nb * c_len // TM

    cvec = jnp.zeros((SUBLANES, d), F32).at[:nb].set(c).at[nb].set(c_ctx)
    mod = _adaln(cvec, ada_w, ada_b, interpret)
    xf = jnp.concatenate([x.reshape(nb * s_len, d), ctx.reshape(nb * c_len, d)], axis=0)

    rope_cos, rope_sin = _rope_tables(s_len)
    rw = jnp.zeros((d, LANES), F32).at[:, :N_EXPERTS].set(router_w)
    rw_hi, rw_lo = _split_bf16(rw)
    rb = router_bias.astype(F32)
    wg_all, wu_all, wd_all = exp_w_gate.astype(BF16), exp_w_up.astype(BF16), exp_w_down.astype(BF16)

    for l in range(depth):
        last = l == depth - 1
        n_tiles = n_lat_tiles if last else n_all_tiles
        mod_l = mod[l]
        if l % 2 == 0:
            e = l // 2
            u_all = _even_in(xf, mod_l, norm_mix_g[l], ev_w_in[e].astype(BF16), dims, interpret)
            p = {
                "conv_w": lru_conv_w[e], "conv_b": lru_conv_b[e].reshape(1, D_LRU),
                "w_r": _block_diag_tiles(lru_w_r[e]), "b_r": lru_b_r[e].reshape(2, 1, D_LRU),
                "w_i": _block_diag_tiles(lru_w_i[e]), "b_i": lru_b_i[e].reshape(2, 1, D_LRU),
                "lam": lru_lambda[e].reshape(2, 1, D_LRU),
                "dw_w": cf_dw_w[e], "dw_b": cf_dw_b[e].reshape(1, D_CONV),
            }
            y_lat, y_ctx = _seq_mix(u_all, p, dims, interpret)
            w_out = ev_w_out[e].astype(BF16)
            ln_g, ln_b = cf_ln_g[e], cf_ln_b[e]
        else:
            o = l // 2
            qkv = _odd_in(xf, mod_l, norm_mix_g[l], at_w_qkv[o].astype(BF16), at_q_norm_g[o], at_k_norm_g[o],
                          rope_cos, rope_sin, dims, interpret)
            y_lat = _attention(qkv, dims, True, interpret)
            y_ctx = y_lat if last else _attention(qkv, dims, False, interpret)
            w_out = at_w_out[o].astype(BF16)
            ln_g, ln_b = cf_ln_g[0], cf_ln_b[0]
        cap = n_tiles * TM + TMS
        x_mid, hrow, pos, counts = _mix_out(y_lat, y_ctx, xf, mod_l, w_out, ln_g, ln_b, norm_ffn_g[l], rw_hi, rw_lo,
                                            rb, dims, n_tiles, cap, l % 2 == 0, interpret)
        xf = _moe(x_mid, hrow, pos, counts, mod_l, wg_all, wu_all, wd_all, l, final_norm_g, dims, cap, last, interpret)
    return xf.reshape(nb, s_len, d)


def kernel(x, c, ctx, c_ctx, ada_w, ada_b, norm_mix_g, norm_ffn_g, ev_w_in, ev_w_out, lru_conv_w, lru_conv_b,
           lru_w_r, lru_b_r, lru_w_i, lru_b_i, lru_lambda, cf_dw_w, cf_dw_b, cf_ln_g, cf_ln_b, at_w_qkv, at_w_out,
           at_q_norm_g, at_k_norm_g, router_w, router_bias, exp_w_gate, exp_w_up, exp_w_down, final_norm_g):
    return _forward(x, c, ctx, c_ctx, ada_w, ada_b, norm_mix_g, norm_ffn_g, ev_w_in, ev_w_out, lru_conv_w,
                    lru_conv_b, lru_w_r, lru_b_r, lru_w_i, lru_b_i, lru_lambda, cf_dw_w, cf_dw_b, cf_ln_g,
                    cf_ln_b, at_w_qkv, at_w_out, at_q_norm_g, at_k_norm_g, router_w, router_bias, exp_w_gate,
                    exp_w_up, exp_w_down, final_norm_g)
```

```python
import functools
import math

import jax
import jax.numpy as jnp
from jax import lax
from jax.experimental import pallas as pl
from jax.experimental.pallas import tpu as pltpu

F32 = jnp.float32
BF16 = jnp.bfloat16

D_MODEL = 1024
N_MOD = 6
NORM_EPS = 1e-6
GRID_W = 64

D_LRU = 512
LRU_HEADS = 8
LRU_HEAD_DIM = D_LRU // LRU_HEADS
LRU_CONV_W = 4
LRU_C = 8.0
D_CONV = 512
CONF_WIDTH = 31
CONF_PAD = 16

HEAD_DIM = 128
N_Q_HEADS = 8
N_KV_HEADS = 2
GQA_GROUP = N_Q_HEADS // N_KV_HEADS
ROPE_THETA = 10000.0

N_EXPERTS = 16
N_GROUPS = 4
EXPERTS_PER_GROUP = 4
D_EXPERT = 512

LANES = 128
SUBLANES = 8
TM = 512
TMS = 256
PAIRS_PER_GROUP = 6
N_CLASSES = N_GROUPS * PAIRS_PER_GROUP
CLASS_ROWS = 32
ISSUE_UNROLL = 8
TD = 1024
ROW_W = D_MODEL + LANES
CT = 128
TQ = 512
TK = 1024
CONV_CHUNK = 128
LRU_CHUNK = 256
CARRY_UNROLL = 4
VMEM_LIMIT = 56 * 1024 * 1024


def _cparams(sem):
    return pltpu.CompilerParams(dimension_semantics=sem, vmem_limit_bytes=VMEM_LIMIT)


def _rms(x, g):
    return x * lax.rsqrt(jnp.mean(x * x, axis=-1, keepdims=True) + NORM_EPS) * g


def _split_bf16(a):
    hi = a.astype(BF16)
    lo = (a - hi.astype(F32)).astype(BF16)
    return hi, lo


def _dot(a, b):
    return jnp.dot(a, b, preferred_element_type=F32)


def _dot3(a_hi, a_lo, b_hi, b_lo):
    return _dot(a_hi, b_hi) + (_dot(a_lo, b_hi) + _dot(a_hi, b_lo))


def _mod_row(i, n_lat_tiles, tiles_per_batch, nb):
    return jnp.where(i < n_lat_tiles, i // tiles_per_batch, nb)


def _adaln_kernel(c_ref, w_ref, b_ref, o_ref):
    cv = c_ref[...]
    s = cv * jax.nn.sigmoid(cv)
    s_hi, s_lo = _split_bf16(s)
    w_hi, w_lo = _split_bf16(w_ref[...])
    o_ref[...] = _dot3(s_hi, s_lo, w_hi, w_lo) + b_ref[...]


def _adaln(cvec, ada_w, ada_b, interpret):
    depth = ada_w.shape[0]
    n_out = ada_w.shape[2]
    tn = D_MODEL
    return pl.pallas_call(
        _adaln_kernel,
        out_shape=jax.ShapeDtypeStruct((depth, SUBLANES, n_out), F32),
        grid=(depth, n_out // tn),
        in_specs=[
            pl.BlockSpec((SUBLANES, D_MODEL), lambda l, j: (0, 0)),
            pl.BlockSpec((None, D_MODEL, tn), lambda l, j: (l, 0, j)),
            pl.BlockSpec((None, 1, tn), lambda l, j: (l, 0, j)),
        ],
        out_specs=pl.BlockSpec((None, SUBLANES, tn), lambda l, j: (l, 0, j)),
        compiler_params=_cparams(("arbitrary", "arbitrary")),
        name="adaln_mod",
        interpret=interpret,
    )(cvec, ada_w, ada_b.reshape(depth, 1, n_out))


def _split_specs(n_lat, n_ctx):
    return [pl.BlockSpec((TM, D_MODEL), lambda i, *_: (jnp.minimum(i, n_lat - 1), 0)),
            pl.BlockSpec((TM, D_MODEL), lambda i, *_: (jnp.clip(i - n_lat, 0, n_ctx - 1), 0))]


def _even_in_kernel(*refs, split_x, n_lat_tiles, tiles_per_batch, nb):
    i = pl.program_id(0)
    if split_x:
        x = jnp.where(i < n_lat_tiles, refs[0][...], refs[1][...])
        refs = refs[2:]
    else:
        x = refs[0][...]
        refs = refs[1:]
    mod_ref, g_ref, w_ref, o_ref = refs
    row = _mod_row(i, n_lat_tiles, tiles_per_batch, nb)
    sh = mod_ref[pl.ds(row, 1), 0:D_MODEL]
    sc = mod_ref[pl.ds(row, 1), D_MODEL:2 * D_MODEL]
    h = _rms(x, g_ref[...]) * (1.0 + sc) + sh
    u = _dot(h.astype(BF16), w_ref[...])
    o_ref[:, 0:D_LRU] = u[:, 0:D_LRU]
    o_ref[:, D_LRU:2 * D_LRU] = jax.nn.gelu(u[:, D_LRU:2 * D_LRU])
    val = u[:, 2 * D_LRU:2 * D_LRU + D_CONV]
    gate = u[:, 2 * D_LRU + D_CONV:]
    o_ref[:, 2 * D_LRU:] = val * jax.nn.sigmoid(gate)


def _even_in(x, mod_l, g, w_in, dims, interpret):
    nb, s_len, c_len = dims
    split_x = isinstance(x, tuple)
    xs = x if split_x else (x,)
    t = sum(a.shape[0] for a in xs)
    n_lat = nb * s_len // TM
    n_out = 2 * D_LRU + D_CONV
    kern = functools.partial(_even_in_kernel, split_x=split_x, n_lat_tiles=n_lat,
                             tiles_per_batch=s_len // TM, nb=nb)
    x_specs = _split_specs(n_lat, nb * c_len // TM) if split_x else [pl.BlockSpec((TM, D_MODEL), lambda i: (i, 0))]
    return pl.pallas_call(
        kern,
        out_shape=jax.ShapeDtypeStruct((t, n_out), F32),
        grid=(t // TM,),
        in_specs=x_specs + [
            pl.BlockSpec(mod_l.shape, lambda i: (0, 0)),
            pl.BlockSpec((1, D_MODEL), lambda i: (0, 0)),
            pl.BlockSpec(w_in.shape, lambda i: (0, 0)),
        ],
        out_specs=pl.BlockSpec((TM, n_out), lambda i: (i, 0)),
        compiler_params=_cparams(("arbitrary",)),
        name="even_in",
        interpret=interpret,
    )(*xs, mod_l, g.reshape(1, D_MODEL), w_in)


def _block_scan(a, b, reverse):
    n, width = a.shape
    a = a.reshape(n // SUBLANES, SUBLANES, width)
    b = b.reshape(n // SUBLANES, SUBLANES, width)
    tmod = lax.broadcasted_iota(jnp.int32, a.shape, 1)
    d = 1
    while d < SUBLANES:
        shift = SUBLANES - d if reverse else d
        keep = (tmod < SUBLANES - d) if reverse else (tmod >= d)
        a_n = pltpu.roll(a, shift, axis=1)
        b_n = pltpu.roll(b, shift, axis=1)
        b = a * jnp.where(keep, b_n, 0.0) + b
        a = a * jnp.where(keep, a_n, 1.0)
        d *= 2
    return a.reshape(n, width), b.reshape(n, width)


def _lru_coeffs(u, ub, wr, br, wi, bi, nsp):
    r = jax.nn.sigmoid(_dot(ub, wr) + br)
    i = jax.nn.sigmoid(_dot(ub, wi) + bi)
    log_a = (-LRU_C) * r * nsp
    a = jnp.exp(log_a)
    b = jnp.sqrt(1.0 - a * a) * (i * u)
    return a, b


def _seq_kernel(xl_ref, xc_ref, gl_ref, gc_ref, vl_ref, vc_ref,
                cw_ref, cb_ref, wr_ref, br_ref, wi_ref, bi_ref, sp_ref, dw_ref, db_ref,
                ol_ref, oc_ref,
                af_ref, bf_ref, ar_ref, br_sc_ref, hf_ref, hr_ref, xp_ref, ph_ref, *, n_lru_tiles):
    c = pl.program_id(1)
    s_len = xl_ref.shape[0]
    c_len = xc_ref.shape[0]

    @pl.when(c < n_lru_tiles)
    def _lru():
        cw = cw_ref[...]
        cb = cb_ref[...]
        nsp = [jax.nn.softplus(-sp_ref[d]) for d in range(2)]

        def coeffs(x_ref, n):
            zpad = jnp.zeros((SUBLANES, CT), F32)
            xp_ref[0:SUBLANES] = zpad
            xp_ref[SUBLANES:SUBLANES + n] = x_ref[...]
            xp_ref[SUBLANES + n:2 * SUBLANES + n] = zpad
            ch = min(LRU_CHUNK, n)

            def body(j, _):
                t0 = pl.multiple_of(j * ch, ch)
                win = xp_ref[pl.ds(t0, ch + 2 * SUBLANES)]
                off = SUBLANES - LRU_CONV_W // 2
                u = cb
                for k in range(LRU_CONV_W):
                    u = u + cw[k:k + 1] * win[off + k:off + k + ch]
                ub = u.astype(BF16)
                a_f, b_f = _lru_coeffs(u, ub, wr_ref[0], br_ref[0], wi_ref[0], bi_ref[0], nsp[0])
                a_f, b_f = _block_scan(a_f, b_f, False)
                af_ref[pl.ds(t0, ch)] = a_f
                bf_ref[pl.ds(t0, ch)] = b_f
                a_r, b_r = _lru_coeffs(u, ub, wr_ref[1], br_ref[1], wi_ref[1], bi_ref[1], nsp[1])
                a_r, b_r = _block_scan(a_r, b_r, True)
                ar_ref[pl.ds(t0, ch)] = a_r
                br_sc_ref[pl.ds(t0, ch)] = b_r
                return 0

            lax.fori_loop(0, n // ch, body, 0)

        def carry_pass(n, h0f, h0r):
            nblk = n // SUBLANES

            def body(j, carry):
                hf, hr = carry
                for u in range(CARRY_UNROLL):
                    k = j * CARRY_UNROLL + u
                    kf = pl.multiple_of(k * SUBLANES, SUBLANES)
                    kr = pl.multiple_of((nblk - 1 - k) * SUBLANES, SUBLANES)
                    yf = bf_ref[pl.ds(kf, SUBLANES)] + af_ref[pl.ds(kf, SUBLANES)] * hf
                    yr = br_sc_ref[pl.ds(kr, SUBLANES)] + ar_ref[pl.ds(kr, SUBLANES)] * hr
                    hf_ref[pl.ds(kf, SUBLANES)] = yf
                    hr_ref[pl.ds(kr, SUBLANES)] = yr
                    hf, hr = yf[SUBLANES - 1:SUBLANES], yr[0:1]
                return hf, hr

            return lax.fori_loop(0, nblk // CARRY_UNROLL, body, (h0f, h0r))

        zero = jnp.zeros((1, CT), F32)
        coeffs(xc_ref, c_len)
        hcf, hcr = carry_pass(c_len, zero, zero)
        oc_ref[...] = (hf_ref[0:c_len] + hr_ref[0:c_len]) * gc_ref[...]
        coeffs(xl_ref, s_len)
        carry_pass(s_len, hcf, hcr)
        ol_ref[...] = (hf_ref[0:s_len] + hr_ref[0:s_len]) * gl_ref[...]

    @pl.when(c >= n_lru_tiles)
    def _conv():
        dw = dw_ref[...]
        db = db_ref[...]
        zpad = jnp.zeros((CONF_PAD, CT), F32)

        def conv(v_ref, o_ref, n):
            xp_ref[0:CONF_PAD] = zpad
            xp_ref[CONF_PAD:CONF_PAD + n] = v_ref[...]
            xp_ref[CONF_PAD + n:2 * CONF_PAD + n] = zpad

            def body(j, _):
                t0 = pl.multiple_of(j * CONV_CHUNK, CONV_CHUNK)
                win = xp_ref[pl.ds(t0, CONV_CHUNK + 2 * CONF_PAD)]
                acc = jnp.broadcast_to(db, (CONV_CHUNK, CT))
                off = CONF_PAD - CONF_WIDTH // 2
                span = CONV_CHUNK + 2 * CONF_PAD - SUBLANES
                for r in range(SUBLANES):
                    ph_ref[r] = win[r:r + span]
                for k in range(CONF_WIDTH):
                    r = (off + k) % SUBLANES
                    q = (off + k) // SUBLANES * SUBLANES
                    acc = acc + dw[k:k + 1] * ph_ref[r, q:q + CONV_CHUNK]
                o_ref[pl.ds(t0, CONV_CHUNK)] = acc
                return 0

            lax.fori_loop(0, n // CONV_CHUNK, body, 0)

        conv(vc_ref, oc_ref, c_len)
        conv(vl_ref, ol_ref, s_len)


def _seq_mix(u_all, p, dims, interpret):
    nb, s_len, c_len = dims
    n_lru = D_LRU // CT
    n_cv = D_CONV // CT
    lat_blocks = nb * s_len // c_len

    def lru_c(c):
        return jnp.minimum(c, n_lru - 1)

    def cv_c(c):
        return jnp.maximum(c - n_lru, 0)

    in_specs = [
        pl.BlockSpec((s_len, CT), lambda b, c: (b, lru_c(c))),
        pl.BlockSpec((c_len, CT), lambda b, c: (lat_blocks + b, lru_c(c))),
        pl.BlockSpec((s_len, CT), lambda b, c: (b, n_lru + lru_c(c))),
        pl.BlockSpec((c_len, CT), lambda b, c: (lat_blocks + b, n_lru + lru_c(c))),
        pl.BlockSpec((s_len, CT), lambda b, c: (b, 2 * n_lru + cv_c(c))),
        pl.BlockSpec((c_len, CT), lambda b, c: (lat_blocks + b, 2 * n_lru + cv_c(c))),
        pl.BlockSpec((LRU_CONV_W, CT), lambda b, c: (0, lru_c(c))),
        pl.BlockSpec((1, CT), lambda b, c: (0, lru_c(c))),
        pl.BlockSpec((2, None, CT, CT), lambda b, c: (0, lru_c(c), 0, 0)),
        pl.BlockSpec((2, 1, CT), lambda b, c: (0, 0, lru_c(c))),
        pl.BlockSpec((2, None, CT, CT), lambda b, c: (0, lru_c(c), 0, 0)),
        pl.BlockSpec((2, 1, CT), lambda b, c: (0, 0, lru_c(c))),
        pl.BlockSpec((2, 1, CT), lambda b, c: (0, 0, lru_c(c))),
        pl.BlockSpec((CONF_WIDTH, CT), lambda b, c: (0, cv_c(c))),
        pl.BlockSpec((1, CT), lambda b, c: (0, cv_c(c))),
    ]
    out_specs = [
        pl.BlockSpec((s_len, CT), lambda b, c: (b, c)),
        pl.BlockSpec((c_len, CT), lambda b, c: (b, c)),
    ]
    scratch = [pltpu.VMEM((s_len, CT), F32) for _ in range(6)]
    scratch.append(pltpu.VMEM((s_len + 2 * CONF_PAD, CT), F32))
    scratch.append(pltpu.VMEM((SUBLANES, CONV_CHUNK + 2 * CONF_PAD - SUBLANES, CT), F32))
    return pl.pallas_call(
        functools.partial(_seq_kernel, n_lru_tiles=n_lru),
        out_shape=[jax.ShapeDtypeStruct((nb * s_len, D_LRU + D_CONV), F32),
                   jax.ShapeDtypeStruct((nb * c_len, D_LRU + D_CONV), F32)],
        grid=(nb, n_lru + n_cv),
        in_specs=in_specs,
        out_specs=out_specs,
        scratch_shapes=scratch,
        compiler_params=_cparams(("arbitrary", "arbitrary")),
        name="seq_mix",
        interpret=interpret,
    )(u_all, u_all, u_all, u_all, u_all, u_all,
      p["conv_w"], p["conv_b"], p["w_r"], p["b_r"], p["w_i"], p["b_i"], p["lam"], p["dw_w"], p["dw_b"])


def _swap32(x):
    lane = lax.broadcasted_iota(jnp.int32, x.shape, 1)
    return jnp.where((lane & 32) == 0, pltpu.roll(x, HEAD_DIM - 32, axis=1), pltpu.roll(x, 32, axis=1))


def _odd_in_kernel(x_ref, mod_ref, g_ref, w_ref, qg_ref, kg_ref, cos_ref, sin_ref, o_ref,
                   *, n_lat_tiles, tiles_per_batch, nb):
    i = pl.program_id(0)
    row = _mod_row(i, n_lat_tiles, tiles_per_batch, nb)
    sh = mod_ref[pl.ds(row, 1), 0:D_MODEL]
    sc = mod_ref[pl.ds(row, 1), D_MODEL:2 * D_MODEL]
    h = _rms(x_ref[...], g_ref[...]) * (1.0 + sc) + sh
    u = _dot(h.astype(BF16), w_ref[...])
    is_lat = i < n_lat_tiles
    cos = jnp.where(is_lat, cos_ref[...], 1.0)
    sin = jnp.where(is_lat, sin_ref[...], 0.0)
    q_scale = math.log2(math.e) / math.sqrt(HEAD_DIM)
    for hh in range(N_Q_HEADS + N_KV_HEADS):
        xh = u[:, hh * HEAD_DIM:(hh + 1) * HEAD_DIM]
        gain = qg_ref[...] if hh < N_Q_HEADS else kg_ref[...]
        xn = _rms(xh, gain)
        y = xn * cos + _swap32(xn) * sin
        if hh < N_Q_HEADS:
            y = y * q_scale
        o_ref[:, hh * HEAD_DIM:(hh + 1) * HEAD_DIM] = y.astype(BF16)
    v0 = (N_Q_HEADS + N_KV_HEADS) * HEAD_DIM
    o_ref[:, v0:] = u[:, v0:].astype(BF16)


def _odd_in(x, mod_l, g, w_qkv, q_g, k_g, rope_cos, rope_sin, dims, interpret):
    nb, s_len, c_len = dims
    t = x.shape[0]
    n_out = w_qkv.shape[1]
    tpb = s_len // TM
    n_lat = nb * tpb
    kern = functools.partial(_odd_in_kernel, n_lat_tiles=n_lat, tiles_per_batch=tpb, nb=nb)

    def pos_block(i):
        return jnp.where(i < n_lat, i % tpb, 0)

    return pl.pallas_call(
        kern,
        out_shape=jax.ShapeDtypeStruct((t, n_out), BF16),
        grid=(t // TM,),
        in_specs=[
            pl.BlockSpec((TM, D_MODEL), lambda i: (i, 0)),
            pl.BlockSpec(mod_l.shape, lambda i: (0, 0)),
            pl.BlockSpec((1, D_MODEL), lambda i: (0, 0)),
            pl.BlockSpec(w_qkv.shape, lambda i: (0, 0)),
            pl.BlockSpec((1, HEAD_DIM), lambda i: (0, 0)),
            pl.BlockSpec((1, HEAD_DIM), lambda i: (0, 0)),
            pl.BlockSpec((TM, HEAD_DIM), lambda i: (pos_block(i), 0)),
            pl.BlockSpec((TM, HEAD_DIM), lambda i: (pos_block(i), 0)),
        ],
        out_specs=pl.BlockSpec((TM, n_out), lambda i: (i, 0)),
        compiler_params=_cparams(("arbitrary",)),
        name="odd_in",
        interpret=interpret,
    )(x, mod_l, g.reshape(1, D_MODEL), w_qkv, q_g.reshape(1, HEAD_DIM), k_g.reshape(1, HEAD_DIM),
      rope_cos, rope_sin)


def _attn_kernel(*refs, n_seg):
    q_ref = refs[0]
    k_refs = refs[1:1 + n_seg]
    v_refs = refs[1 + n_seg:1 + 2 * n_seg]
    o_ref = refs[1 + 2 * n_seg]
    tq = q_ref.shape[0]
    chunks = []
    for k_ref, v_ref in zip(k_refs, v_refs):
        n = k_ref.shape[0]
        step = min(TK, n)
        for s0 in range(0, n, step):
            chunks.append((k_ref, v_ref, s0, step))
    for g in range(GQA_GROUP):
        q = q_ref[:, g * HEAD_DIM:(g + 1) * HEAD_DIM]
        m = jnp.full((tq, 1), -jnp.inf, F32)
        acc = jnp.zeros((tq, 2 * HEAD_DIM), F32)
        for k_ref, v_ref, s0, step in chunks:
            k = k_ref[s0:s0 + step, :]
            v1 = jnp.concatenate([v_ref[s0:s0 + step, :], jnp.ones((step, HEAD_DIM), BF16)], axis=1)
            s = lax.dot_general(q, k, (((1,), (1,)), ((), ())), preferred_element_type=F32)
            m_new = jnp.maximum(m, jnp.max(s, axis=-1, keepdims=True))
            p = jnp.exp2(s - m_new)
            acc = jnp.exp2(m - m_new) * acc + _dot(p.astype(BF16), v1)
            m = m_new
        o_ref[:, g * HEAD_DIM:(g + 1) * HEAD_DIM] = (acc[:, 0:HEAD_DIM] / acc[:, HEAD_DIM:HEAD_DIM + 1]).astype(BF16)


def _attention(qkv, dims, latent, interpret):
    nb, s_len, c_len = dims
    k_col = N_Q_HEADS
    v_col = N_Q_HEADS + N_KV_HEADS
    lat_blocks = nb * s_len // c_len
    ctx_k = pl.BlockSpec((c_len, HEAD_DIM), lambda b, h, qi: (lat_blocks + b, k_col + h))
    ctx_v = pl.BlockSpec((c_len, HEAD_DIM), lambda b, h, qi: (lat_blocks + b, v_col + h))
    if latent:
        tq = TQ
        nq = s_len // tq
        q_spec = pl.BlockSpec((tq, GQA_GROUP * HEAD_DIM), lambda b, h, qi: (b * nq + qi, h))
        k_specs = [pl.BlockSpec((s_len, HEAD_DIM), lambda b, h, qi: (b, k_col + h)), ctx_k]
        v_specs = [pl.BlockSpec((s_len, HEAD_DIM), lambda b, h, qi: (b, v_col + h)), ctx_v]
        out_rows = nb * s_len
        o_spec = pl.BlockSpec((tq, GQA_GROUP * HEAD_DIM), lambda b, h, qi: (b * nq + qi, h))
    else:
        tq = c_len
        nq = 1
        q_spec = pl.BlockSpec((tq, GQA_GROUP * HEAD_DIM), lambda b, h, qi: (lat_blocks + b, h))
        k_specs = [ctx_k]
        v_specs = [ctx_v]
        out_rows = nb * c_len
        o_spec = pl.BlockSpec((tq, GQA_GROUP * HEAD_DIM), lambda b, h, qi: (b, h))
    n_seg = len(k_specs)
    return pl.pallas_call(
        functools.partial(_attn_kernel, n_seg=n_seg),
        out_shape=jax.ShapeDtypeStruct((out_rows, N_Q_HEADS * HEAD_DIM), BF16),
        grid=(nb, N_KV_HEADS, nq),
        in_specs=[q_spec] + k_specs + v_specs,
        out_specs=o_spec,
        compiler_params=_cparams(("arbitrary", "arbitrary", "arbitrary")),
        name="attn_lat" if latent else "attn_ctx",
        interpret=interpret,
    )(*([qkv] * (1 + 2 * n_seg)))


def _route(logits_t, bias):
    score = [jax.nn.sigmoid(logits_t[e:e + 1]) for e in range(N_EXPERTS)]
    sel = [score[e] + bias[e] for e in range(N_EXPERTS)]
    grp = []
    for g in range(N_GROUPS):
        a, b, c, d = sel[4 * g:4 * g + 4]
        hi1, lo1 = jnp.maximum(a, b), jnp.minimum(a, b)
        hi2, lo2 = jnp.maximum(c, d), jnp.minimum(c, d)
        top1 = jnp.maximum(hi1, hi2)
        top2 = jnp.maximum(jnp.minimum(hi1, hi2), jnp.maximum(lo1, lo2))
        grp.append(top1 + top2)
    best = jnp.zeros_like(grp[0], dtype=jnp.int32)
    best_s = grp[0]
    for g in range(1, N_GROUPS):
        better = grp[g] > best_s
        best = jnp.where(better, g, best)
        best_s = jnp.where(better, grp[g], best_s)
    chosen = []
    for e in range(N_EXPERTS):
        g = e // EXPERTS_PER_GROUP
        beaten = jnp.zeros_like(best)
        for k in range(4 * g, 4 * g + 4):
            if k == e:
                continue
            beats = (sel[k] >= sel[e]) if k < e else (sel[k] > sel[e])
            beaten = beaten + jnp.where(beats, 1, 0)
        chosen.append((beaten < 2) & (best == g))
    wsum = jnp.zeros_like(score[0])
    for e in range(N_EXPERTS):
        wsum = wsum + jnp.where(chosen[e], score[e], 0.0)
    picked, gate = [], []
    for j in range(EXPERTS_PER_GROUP):
        cj = jnp.zeros_like(best)
        gj = jnp.zeros_like(wsum)
        for g in range(N_GROUPS):
            e = EXPERTS_PER_GROUP * g + j
            cj = jnp.where(chosen[e], 1, cj)
            gj = jnp.where(best == g, score[e], gj)
        picked.append(cj > 0)
        gate.append(gj / wsum)
    j_lo = jnp.where(picked[0], 0, jnp.where(picked[1], 1, 2))
    j_hi = jnp.where(picked[3], 3, jnp.where(picked[2], 2, 1))
    pair = jnp.where(j_lo == 0, j_hi - 1, jnp.where(j_lo == 1, j_hi + 1, PAIRS_PER_GROUP - 1))
    gate_lo = jnp.where(j_lo == 0, gate[0], jnp.where(j_lo == 1, gate[1], gate[2]))
    gate_hi = jnp.where(j_hi == 3, gate[3], jnp.where(j_hi == 2, gate[2], gate[1]))
    return best * PAIRS_PER_GROUP + pair, gate_lo, gate_hi


def _mix_out_kernel(yl_ref, yc_ref, *refs, split_x, even, n_lat_tiles, tiles_per_batch, nb, cap):
    i = pl.program_id(0)
    if split_x:
        x_in = jnp.where(i < n_lat_tiles, refs[0][...], refs[1][...])
        refs = refs[2:]
    else:
        x_in = refs[0][...]
        refs = refs[1:]
    (mod_ref, w_ref, lng_ref, lnb_ref, nfg_ref, rwh_ref, rwl_ref, rb_ref,
     xo_ref, hrow_ref, pos_ref, cnt_ref, base_ref) = refs

    @pl.when(i == 0)
    def _init():
        base_ref[...] = jnp.zeros_like(base_ref)

    row = _mod_row(i, n_lat_tiles, tiles_per_batch, nb)
    g1 = mod_ref[pl.ds(row, 1), 2 * D_MODEL:3 * D_MODEL]
    sh2 = mod_ref[pl.ds(row, 1), 3 * D_MODEL:4 * D_MODEL]
    sc2 = mod_ref[pl.ds(row, 1), 4 * D_MODEL:5 * D_MODEL]
    is_lat = i < n_lat_tiles
    y = jnp.where(is_lat, yl_ref[...], yc_ref[...])
    if even:
        ya = y[:, 0:D_LRU]
        vc = y[:, D_LRU:]
        mu = jnp.mean(vc, axis=-1, keepdims=True)
        xc = vc - mu
        var = jnp.mean(xc * xc, axis=-1, keepdims=True)
        ln = xc * lax.rsqrt(var + NORM_EPS) * lng_ref[...] + lnb_ref[...]
        yb = ln * jax.nn.sigmoid(ln)
        out = _dot(ya.astype(BF16), w_ref[0:D_LRU, :]) + _dot(yb.astype(BF16), w_ref[D_LRU:, :])
    else:
        out = _dot(y, w_ref[...])
    x_new = x_in + g1 * out
    xo_ref[...] = x_new
    h2 = _rms(x_new, nfg_ref[...]) * (1.0 + sc2) + sh2
    hrow_ref[:, 0:D_MODEL] = h2
    h_hi, h_lo = _split_bf16(h2)
    logits = _dot3(h_hi, h_lo, rwh_ref[...], rwl_ref[...])
    logits_t = jnp.transpose(logits)[0:N_EXPERTS]
    cls, gate_lo, gate_hi = _route(logits_t, [rb_ref[e] for e in range(N_EXPERTS)])
    tm = logits.shape[0]

    sub = lax.broadcasted_iota(jnp.int32, (CLASS_ROWS, tm), 0)
    onehot = jnp.where(sub == cls, 1.0, 0.0)
    earlier = (lax.broadcasted_iota(jnp.int32, (tm, tm), 0) < lax.broadcasted_iota(jnp.int32, (tm, tm), 1))
    before = _dot(onehot.astype(BF16), jnp.where(earlier, 1.0, 0.0).astype(BF16))
    rank = jnp.sum(onehot * (before + base_ref[:, 0:1]), axis=0, keepdims=True)
    pos_ref[...] = cls * cap + rank.astype(jnp.int32)
    base_ref[...] = base_ref[...] + jnp.sum(onehot, axis=1, keepdims=True)
    cnt_ref[...] = base_ref[...]

    sub8 = lax.broadcasted_iota(jnp.int32, (SUBLANES, tm), 0)
    extras_t = jnp.where(sub8 == 0, gate_lo, jnp.where(sub8 == 1, gate_hi, 0.0))
    extras_t = jnp.concatenate([extras_t, jnp.zeros((LANES - SUBLANES, tm), F32)], axis=0)
    hrow_ref[:, D_MODEL:] = jnp.transpose(extras_t)


def _mix_out(y_lat, y_ctx, x, mod_l, w_out, ln_g, ln_b, nf_g, rw_hi, rw_lo, router_bias, dims, n_tiles, cap, even,
             interpret):
    nb, s_len, c_len = dims
    tpb = s_len // TM
    n_lat = nb * tpb
    n_ctx = max(nb * c_len // TM, 1)
    rows = n_tiles * TM
    split_x = isinstance(x, tuple)
    xs = x if split_x else (x,)
    x_specs = _split_specs(n_lat, n_ctx) if split_x else [pl.BlockSpec((TM, D_MODEL), lambda i: (i, 0))]
    kern = functools.partial(_mix_out_kernel, split_x=split_x, even=even, n_lat_tiles=n_lat, tiles_per_batch=tpb,
                             nb=nb, cap=cap)
    vec = lambda n: pl.BlockSpec((1, n), lambda i: (0, 0))
    return pl.pallas_call(
        kern,
        out_shape=[jax.ShapeDtypeStruct((rows, D_MODEL), F32),
                   jax.ShapeDtypeStruct((rows, ROW_W), F32),
                   jax.ShapeDtypeStruct((n_tiles, 1, TM), jnp.int32),
                   jax.ShapeDtypeStruct((CLASS_ROWS, LANES), F32)],
        grid=(n_tiles,),
        in_specs=[
            pl.BlockSpec((TM, D_MODEL), lambda i: (jnp.minimum(i, n_lat - 1), 0)),
            pl.BlockSpec((TM, D_MODEL), lambda i: (jnp.clip(i - n_lat, 0, n_ctx - 1), 0)),
        ] + x_specs + [
            pl.BlockSpec(mod_l.shape, lambda i: (0, 0)),
            pl.BlockSpec(w_out.shape, lambda i: (0, 0)),
            vec(D_CONV), vec(D_CONV), vec(D_MODEL),
            pl.BlockSpec(rw_hi.shape, lambda i: (0, 0)),
            pl.BlockSpec(rw_lo.shape, lambda i: (0, 0)),
            pl.BlockSpec(memory_space=pltpu.SMEM),
        ],
        out_specs=[
            pl.BlockSpec((TM, D_MODEL), lambda i: (i, 0)),
            pl.BlockSpec((TM, ROW_W), lambda i: (i, 0)),
            pl.BlockSpec((None, 1, TM), lambda i: (i, 0, 0)),
            pl.BlockSpec((CLASS_ROWS, LANES), lambda i: (0, 0)),
        ],
        scratch_shapes=[pltpu.VMEM((CLASS_ROWS, LANES), F32)],
        compiler_params=_cparams(("arbitrary",)),
        name="mix_out_even" if even else "mix_out_odd",
        interpret=interpret,
    )(y_lat, y_ctx, *xs, mod_l, w_out, ln_g.reshape(1, D_CONV), ln_b.reshape(1, D_CONV),
      nf_g.reshape(1, D_MODEL), rw_hi, rw_lo, router_bias)


def _row_copy(src_ref, src_row, dst_ref, dst_row, sem):
    return pltpu.make_async_copy(src_ref.at[pl.ds(src_row, 1)], dst_ref.at[pl.ds(dst_row, 1)], sem)


def _dispatch_kernel(pos_ref, fill_ref, nact_ref, h_ref, xs_ref, zero_ref, row_sem, fill_sem, tail_sem, *, n_slots):
    i = pl.program_id(0)
    tm = h_ref.shape[0]

    @pl.when(i == 0)
    def _fill():
        zero_ref[...] = jnp.zeros_like(zero_ref)
        def fill(tile, sem):
            return pltpu.make_async_copy(zero_ref, xs_ref.at[pl.ds(pl.multiple_of(tile * TMS, TMS), TMS)], sem)

        for phase in ("start", "wait"):
            for c in range(N_CLASSES):
                @pl.when(fill_ref[c] >= 0)
                def _last_tile():
                    getattr(fill(fill_ref[c], fill_sem.at[c]), phase)()

                @pl.when(n_slots - 1 - c >= nact_ref[0])
                def _unused_tile():
                    getattr(fill(n_slots - 1 - c, tail_sem.at[c]), phase)()

    def issue(j, _):
        for k in range(ISSUE_UNROLL):
            r = j * ISSUE_UNROLL + k
            _row_copy(h_ref, r, xs_ref, pos_ref[i * tm + r], row_sem).start()
        return 0

    lax.fori_loop(0, tm // ISSUE_UNROLL, issue, 0)
    pltpu.make_async_copy(h_ref, xs_ref.at[pl.ds(0, tm)], row_sem).wait()


def _dispatch(hrow, pos_flat, fill_tile, n_active, n_slots, td, interpret):
    rows = hrow.shape[0]
    return pl.pallas_call(
        functools.partial(_dispatch_kernel, n_slots=n_slots),
        out_shape=jax.ShapeDtypeStruct((n_slots * TMS, ROW_W), F32),
        grid_spec=pltpu.PrefetchScalarGridSpec(
            num_scalar_prefetch=3,
            grid=(rows // td,),
            in_specs=[pl.BlockSpec((td, ROW_W), lambda i, pos, fill, nact: (i, 0))],
            out_specs=pl.BlockSpec(memory_space=pl.ANY),
            scratch_shapes=[pltpu.VMEM((TMS, ROW_W), F32), pltpu.SemaphoreType.DMA,
                            pltpu.SemaphoreType.DMA((N_CLASSES,)), pltpu.SemaphoreType.DMA((N_CLASSES,))],
        ),
        compiler_params=_cparams(("arbitrary",)),
        name="moe_dispatch",
        interpret=interpret,
    )(pos_flat, fill_tile, n_active, hrow)


def _experts_kernel(elo_ref, ehi_ref, nact_ref, xs_ref, wg0_ref, wu0_ref, wd0_ref, wg1_ref, wu1_ref, wd1_ref,
                    ys_ref):
    i = pl.program_id(0)

    @pl.when(i < nact_ref[0])
    def _active():
        h = xs_ref[:, 0:D_MODEL].astype(BF16)
        out = None
        for lane, (wg_ref, wu_ref, wd_ref) in enumerate(((wg0_ref, wu0_ref, wd0_ref),
                                                         (wg1_ref, wu1_ref, wd1_ref))):
            a = _dot(h, wg_ref[...])
            u = _dot(h, wu_ref[...])
            hid = (a * jax.nn.sigmoid(a)) * u * xs_ref[:, D_MODEL + lane:D_MODEL + lane + 1]
            y = _dot(hid.astype(BF16), wd_ref[...])
            out = y if out is None else out + y
        ys_ref[...] = out

    @pl.when(i >= nact_ref[0])
    def _unused():
        ys_ref[...] = jnp.zeros_like(ys_ref)


def _experts(xs, tile_lo, tile_hi, n_active, wg, wu, wd, layer, interpret):
    n_slots = tile_lo.shape[0]
    lo_idx = lambda i, lo, hi, nact: (layer, lo[i], 0, 0)
    hi_idx = lambda i, lo, hi, nact: (layer, hi[i], 0, 0)
    w_in = (None, None, D_MODEL, D_EXPERT)
    w_out = (None, None, D_EXPERT, D_MODEL)
    return pl.pallas_call(
        _experts_kernel,
        out_shape=jax.ShapeDtypeStruct((n_slots * TMS, D_MODEL), F32),
        grid_spec=pltpu.PrefetchScalarGridSpec(
            num_scalar_prefetch=3,
            grid=(n_slots,),
            in_specs=[
                pl.BlockSpec((TMS, ROW_W), lambda i, lo, hi, nact: (jnp.minimum(i, nact[0] - 1), 0)),
                pl.BlockSpec(w_in, lo_idx), pl.BlockSpec(w_in, lo_idx), pl.BlockSpec(w_out, lo_idx),
                pl.BlockSpec(w_in, hi_idx), pl.BlockSpec(w_in, hi_idx), pl.BlockSpec(w_out, hi_idx),
            ],
            out_specs=pl.BlockSpec((TMS, D_MODEL), lambda i, lo, hi, nact: (i, 0)),
        ),
        compiler_params=_cparams(("arbitrary",)),
        name="moe_experts",
        interpret=interpret,
    )(tile_lo, tile_hi, n_active, xs, wg, wu, wd, wg, wu, wd)


def _combine_kernel(pos_ref, ys_ref, x_ref, mod_ref, fg_ref, o_ref, ybuf_ref, sem,
                    *, final, n_lat_tiles, tiles_per_batch, nb):
    i = pl.program_id(0)
    tm = x_ref.shape[0]

    def issue(j, _):
        for k in range(ISSUE_UNROLL):
            r = j * ISSUE_UNROLL + k
            _row_copy(ys_ref, pos_ref[i * tm + r], ybuf_ref, r, sem).start()
        return 0

    lax.fori_loop(0, tm // ISSUE_UNROLL, issue, 0)
    pltpu.make_async_copy(ys_ref.at[pl.ds(0, tm)], ybuf_ref, sem).wait()
    row = _mod_row(i, n_lat_tiles, tiles_per_batch, nb)
    g2 = mod_ref[pl.ds(row, 1), 5 * D_MODEL:6 * D_MODEL]
    x_new = x_ref[...] + g2 * ybuf_ref[...]
    if final:
        x_new = _rms(x_new, fg_ref[...])
    o_ref[...] = x_new


def _combine(ys, pos_flat, x, mod_l, final_g, dims, td, final, interpret):
    nb, s_len, c_len = dims
    rows = x.shape[0]
    tpb = s_len // td
    kern = functools.partial(_combine_kernel, final=final, n_lat_tiles=nb * tpb, tiles_per_batch=tpb, nb=nb)
    return pl.pallas_call(
        kern,
        out_shape=jax.ShapeDtypeStruct((rows, D_MODEL), F32),
        grid_spec=pltpu.PrefetchScalarGridSpec(
            num_scalar_prefetch=1,
            grid=(rows // td,),
            in_specs=[
                pl.BlockSpec(memory_space=pl.ANY),
                pl.BlockSpec((td, D_MODEL), lambda i, pos: (i, 0)),
                pl.BlockSpec(mod_l.shape, lambda i, pos: (0, 0)),
                pl.BlockSpec((1, D_MODEL), lambda i, pos: (0, 0)),
            ],
            out_specs=pl.BlockSpec((td, D_MODEL), lambda i, pos: (i, 0)),
            scratch_shapes=[pltpu.VMEM((td, D_MODEL), F32), pltpu.SemaphoreType.DMA],
        ),
        compiler_params=_cparams(("arbitrary",)),
        name="moe_combine_final" if final else "moe_combine",
        interpret=interpret,
    )(pos_flat, ys, x, mod_l, final_g.reshape(1, D_MODEL))


def _sorted_layout(pos, counts, cap, n_slots):
    cnt = counts[0:N_CLASSES, 0].astype(jnp.int32)
    tiles = (cnt + TMS - 1) // TMS
    ends = jnp.cumsum(tiles)
    starts = (ends - tiles) * TMS
    n_active = ends[-1]
    pos_flat = pos.reshape(-1)
    cls = pos_flat // cap
    classes = jnp.arange(N_CLASSES, dtype=jnp.int32)
    start_of = jnp.sum(jnp.where(cls[:, None] == classes[None, :], starts[None, :], 0), axis=1)
    pos_sorted = start_of + pos_flat % cap
    fill_tile = jnp.where(tiles > 0, ends - 1, -1)
    slot = jnp.minimum(jnp.arange(n_slots, dtype=jnp.int32), n_active - 1)
    tile_cls = jnp.sum((slot[:, None] >= ends[None, :]).astype(jnp.int32), axis=1)
    pair_lo = jnp.array([0, 0, 0, 1, 1, 2], jnp.int32)
    pair_hi = jnp.array([1, 2, 3, 2, 3, 3], jnp.int32)
    first = (tile_cls // PAIRS_PER_GROUP) * EXPERTS_PER_GROUP
    tile_lo = first + jnp.take(pair_lo, tile_cls % PAIRS_PER_GROUP)
    tile_hi = first + jnp.take(pair_hi, tile_cls % PAIRS_PER_GROUP)
    i32 = lambda a: a.astype(jnp.int32)
    return i32(pos_sorted), i32(fill_tile), i32(tile_lo), i32(tile_hi), i32(n_active.reshape(1))


def _moe(x_mid, hrow, pos, counts, mod_l, wg, wu, wd, layer, final_g, dims, cap, final, interpret):
    rows = x_mid.shape[0]
    n_slots = rows // TMS + N_CLASSES
    pos_sorted, fill_tile, tile_lo, tile_hi, n_active = _sorted_layout(pos, counts, cap, n_slots)
    nb, s_len, c_len = dims
    td = TD if (s_len % TD == 0 and (nb * c_len) % TD == 0) else TM
    xs = _dispatch(hrow, pos_sorted, fill_tile, n_active, n_slots, td, interpret)
    ys = _experts(xs, tile_lo, tile_hi, n_active, wg, wu, wd, layer, interpret)
    return _combine(ys, pos_sorted, x_mid, mod_l, final_g, dims, td, final, interpret)


def _rope_tables(s_len):
    rows = s_len // GRID_W
    quarter = HEAD_DIM // 4
    r, col = jnp.meshgrid(jnp.arange(rows, dtype=F32), jnp.arange(GRID_W, dtype=F32), indexing="ij")
    inv_freq = ROPE_THETA ** (-jnp.arange(0, HEAD_DIM // 2, 2, dtype=F32) / (HEAD_DIM // 2))
    ang_r = r.reshape(-1)[:, None] * inv_freq
    ang_c = col.reshape(-1)[:, None] * inv_freq
    cos = jnp.concatenate([jnp.cos(ang_r)] * 2 + [jnp.cos(ang_c)] * 2, axis=-1)
    sin = jnp.concatenate([-jnp.sin(ang_r), jnp.sin(ang_r), -jnp.sin(ang_c), jnp.sin(ang_c)], axis=-1)
    assert cos.shape == (s_len, 4 * quarter)
    return cos, sin


def _block_diag_tiles(w):
    per = CT // LRU_HEAD_DIM
    n_tiles = D_LRU // CT
    w = w.reshape(2, n_tiles, per, LRU_HEAD_DIM, LRU_HEAD_DIM)
    eye = jnp.eye(per, dtype=w.dtype)
    full = jnp.einsum("dtpij,pq->dtpiqj", w, eye)
    return full.reshape(2, n_tiles, CT, CT).astype(BF16)


def _forward(x, c, ctx, c_ctx, ada_w, ada_b, norm_mix_g, norm_ffn_g, ev_w_in, ev_w_out,
             lru_conv_w, lru_conv_b, lru_w_r, lru_b_r, lru_w_i, lru_b_i, lru_lambda,
             cf_dw_w, cf_dw_b, cf_ln_g, cf_ln_b, at_w_qkv, at_w_out, at_q_norm_g, at_k_norm_g,
             router_w, router_bias, exp_w_gate, exp_w_up, exp_w_down, final_norm_g, interpret=False):
    nb, s_len, d = x.shape
    c_len = ctx.shape[1]
    depth = ada_w.shape[0]
    dims = (nb, s_len, c_len)
    assert d == D_MODEL and nb < SUBLANES
    assert s_len % TM == 0 and (nb * c_len) % TM == 0 and s_len % c_len == 0 and s_len % TQ == 0
    n_lat_tiles = nb * s_len // TM
    n_all_tiles = n_lat_tiles + nb * c_len // TM

    cvec = jnp.zeros((SUBLANES, d), F32).at[:nb].set(c).at[nb].set(c_ctx)
    mod = _adaln(cvec, ada_w, ada_b, interpret)
    xf = (x.reshape(nb * s_len, d), ctx.reshape(nb * c_len, d))

    rope_cos, rope_sin = _rope_tables(s_len)
    rw = jnp.zeros((d, LANES), F32).at[:, :N_EXPERTS].set(router_w)
    rw_hi, rw_lo = _split_bf16(rw)
    rb = router_bias.astype(F32)
    wg_all, wu_all, wd_all = exp_w_gate.astype(BF16), exp_w_up.astype(BF16), exp_w_down.astype(BF16)

    for l in range(depth):
        last = l == depth - 1
        n_tiles = n_lat_tiles if last else n_all_tiles
        mod_l = mod[l]
        if l % 2 == 0:
            e = l // 2
            u_all = _even_in(xf, mod_l, norm_mix_g[l], ev_w_in[e].astype(BF16), dims, interpret)
            p = {
                "conv_w": lru_conv_w[e], "conv_b": lru_conv_b[e].reshape(1, D_LRU),
                "w_r": _block_diag_tiles(lru_w_r[e]), "b_r": lru_b_r[e].reshape(2, 1, D_LRU),
                "w_i": _block_diag_tiles(lru_w_i[e]), "b_i": lru_b_i[e].reshape(2, 1, D_LRU),
                "lam": lru_lambda[e].reshape(2, 1, D_LRU),
                "dw_w": cf_dw_w[e], "dw_b": cf_dw_b[e].reshape(1, D_CONV),
            }
            y_lat, y_ctx = _seq_mix(u_all, p, dims, interpret)
            w_out = ev_w_out[e].astype(BF16)
            ln_g, ln_b = cf_ln_g[e], cf_ln_b[e]
        else:
            o = l // 2
            qkv = _odd_in(xf, mod_l, norm_mix_g[l], at_w_qkv[o].astype(BF16), at_q_norm_g[o], at_k_norm_g[o],
                          rope_cos, rope_sin, dims, interpret)
            y_lat = _attention(qkv, dims, True, interpret)
            y_ctx = y_lat if last else _attention(qkv, dims, False, interpret)
            w_out = at_w_out[o].astype(BF16)
            ln_g, ln_b = cf_ln_g[0], cf_ln_b[0]
        cap = n_tiles * TM + TMS
        x_mid, hrow, pos, counts = _mix_out(y_lat, y_ctx, xf, mod_l, w_out, ln_g, ln_b, norm_ffn_g[l], rw_hi, rw_lo,
                                            rb, dims, n_tiles, cap, l % 2 == 0, interpret)
        xf = _moe(x_mid, hrow, pos, counts, mod_l, wg_all, wu_all, wd_all, l, final_norm_g, dims, cap, last, interpret)
    return xf.reshape(nb, s_len, d)


def kernel(x, c, ctx, c_ctx, ada_w, ada_b, norm_mix_g, norm_ffn_g, ev_w_in, ev_w_out, lru_conv_w, lru_conv_b,
           lru_w_r, lru_b_r, lru_w_i, lru_b_i, lru_lambda, cf_dw_w, cf_dw_b, cf_ln_g, cf_ln_b, at_w_qkv, at_w_out,
           at_q_norm_g, at_k_norm_g, router_w, router_bias, exp_w_gate, exp_w_up, exp_w_down, final_norm_g):
    return _forward(x, c, ctx, c_ctx, ada_w, ada_b, norm_mix_g, norm_ffn_g, ev_w_in, ev_w_out, lru_conv_w,
                    lru_conv_b, lru_w_r, lru_b_r, lru_w_i, lru_b_i, lru_lambda, cf_dw_w, cf_dw_b, cf_ln_g,
                    cf_ln_b, at_w_qkv, at_w_out, at_q_norm_g, at_k_norm_g, router_w, router_bias, exp_w_gate,
                    exp_w_up, exp_w_down, final_norm_g)
```
